```python
import math
import jax
import jax.numpy as jnp
from jax import lax
import numpy as np

D_MODEL = 2048
BATCH = 8
SEQ = 4096
DEPTH = 4

CTX_LEN = 256
GRID_W = 64
ROPE_THETA = 10000.0
NORM_EPS = 1e-6
NEG_INF = -1e30

DA_HEADS = 6
DA_QK = 64
DA_V = 2 * DA_QK
DA_Q_BLOCK = 128
DA_WIDTH = DA_HEADS * DA_V

SW_HEADS = 6
SW_KV_HEADS = 2
SW_HD = 128
SW_WINDOW = 128
SW_BLOCK = 128
SW_WIDTH = SW_HEADS * SW_HD

SSM_GROUPS = 32
SSM_GROUP_CH = 16
SSM_STATE = 64
SSM_WIDTH = SSM_GROUPS * SSM_GROUP_CH
SSM_DT_MIN = 1e-3
SSM_DT_MAX = 1e-1
SSM_RE_MAX = -1e-4

MIX_WIDTH = DA_WIDTH + SW_WIDTH + SSM_WIDTH
DA_COLS = 3 * DA_WIDTH
SW_COLS = SW_WIDTH + 2 * SW_KV_HEADS * SW_HD
IN_COLS = DA_COLS + SW_COLS + SSM_WIDTH

N_EXPERTS = 16
EXPERT_FF = 1024
EC_CAPACITY = 2
N_MOD = 6

kernel_name = 'hybrid_diffusion_parallel_heads_ec_moe'


def rms_norm(x, g):
    xf = x.astype(jnp.float32)
    y = xf * lax.rsqrt(jnp.mean(xf * xf, axis=-1, keepdims=True) + NORM_EPS)
    return (y * g.astype(jnp.float32)).astype(x.dtype)


def modulate(h, shift, scale):
    return h * (1.0 + scale) + shift


def axial_rope_tables(rows, head_dim):
    n_freq = head_dim // 4
    inv_freq = ROPE_THETA ** (-jnp.arange(n_freq, dtype=jnp.float32) / n_freq)
    r_idx, c_idx = jnp.meshgrid(jnp.arange(rows, dtype=jnp.float32),
                                jnp.arange(GRID_W, dtype=jnp.float32), indexing='ij')
    ang_r = r_idx.reshape(-1)[:, None] * inv_freq[None, :]
    ang_c = c_idx.reshape(-1)[:, None] * inv_freq[None, :]
    ang = jnp.concatenate([ang_r, ang_r, ang_c, ang_c], axis=-1)
    return jnp.cos(ang), jnp.sin(ang)


def apply_axial_rope(x, cos, sin):
    xf = x.astype(jnp.float32)
    x1, x2, x3, x4 = jnp.split(xf, 4, axis=-1)
    rot = jnp.concatenate([-x2, x1, -x4, x3], axis=-1)
    return (xf * cos[None, :, None, :] + rot * sin[None, :, None, :]).astype(x.dtype)


def _diff_core(q, k, v, lam):
    s = jnp.einsum('bqhmd,bkhmd->bhmqk', q, k).astype(jnp.float32) * (DA_QK ** -0.5)
    p = jax.nn.softmax(s, axis=-1)
    w = p[:, :, 0] - lam * p[:, :, 1]
    return jnp.einsum('bhqk,bkhe->bqhe', w.astype(v.dtype), v)


def diff_attention(ql, kl, vl, qc, kc, vc, lam, with_ctx_out):
    B, L = ql.shape[:2]
    nb = L // DA_Q_BLOCK
    keys = jnp.concatenate([kl, kc], axis=1)
    vals = jnp.concatenate([vl, vc], axis=1)
    qb = jnp.moveaxis(ql.reshape(B, nb, DA_Q_BLOCK, DA_HEADS, 2, DA_QK), 1, 0)
    out = lax.map(lambda qblk: _diff_core(qblk, keys, vals, lam), qb)
    out_l = jnp.moveaxis(out, 0, 1).reshape(B, L, DA_HEADS, DA_V)
    out_c = _diff_core(qc, kc, vc, lam) if with_ctx_out else None
    return out_l, out_c


def window_gqa(ql, kl, vl, qc, kc, vc, sink, with_ctx_out):
    B, L = ql.shape[:2]
    Lc = kc.shape[1]
    grp = SW_HEADS // SW_KV_HEADS
    nb = L // SW_BLOCK
    nj = 3 * SW_BLOCK
    scale = SW_HD ** -0.5
    sink_f = sink.astype(jnp.float32).reshape(SW_KV_HEADS, grp)
    qg = ql.reshape(B, nb, SW_BLOCK, SW_KV_HEADS, grp, SW_HD)
    pad = ((0, 0), (SW_BLOCK, SW_BLOCK), (0, 0), (0, 0))
    kp = jnp.pad(kl, pad).reshape(B, nb + 2, SW_BLOCK, SW_KV_HEADS, SW_HD)
    vp = jnp.pad(vl, pad).reshape(B, nb + 2, SW_BLOCK, SW_KV_HEADS, SW_HD)
    kb = jnp.concatenate([kp[:, :-2], kp[:, 1:-1], kp[:, 2:]], axis=2)
    vb = jnp.concatenate([vp[:, :-2], vp[:, 1:-1], vp[:, 2:]], axis=2)
    s_band = jnp.einsum('bnqhgd,bnjhd->bnhgqj', qg, kb).astype(jnp.float32) * scale
    q_off = jnp.arange(SW_BLOCK)[:, None]
    j_off = jnp.arange(nj)[None, :]
    k_pos = jnp.arange(nb)[:, None, None] * SW_BLOCK - SW_BLOCK + j_off[None]
    band = (jnp.abs(q_off + SW_BLOCK - j_off)[None] <= SW_WINDOW) & (k_pos >= 0) & (k_pos < L)
    s_band = jnp.where(band[None, :, None, None], s_band, NEG_INF)
    s_ctx = jnp.einsum('bnqhgd,bchd->bnhgqc', qg, kc).astype(jnp.float32) * scale
    s_sink = jnp.broadcast_to(sink_f[None, None, :, :, None, None], s_band.shape[:-1] + (1,))
    p = jax.nn.softmax(jnp.concatenate([s_band, s_ctx, s_sink], axis=-1), axis=-1)
    p_band = p[..., :nj].astype(vl.dtype)
    p_ctx = p[..., nj:nj + Lc].astype(vl.dtype)
    out = (jnp.einsum('bnhgqj,bnjhd->bnqhgd', p_band, vb)
           + jnp.einsum('bnhgqc,bchd->bnqhgd', p_ctx, vc))
    out_l = out.reshape(B, L, SW_HEADS, SW_HD)
    out_c = None
    if with_ctx_out:
        qcg = qc.reshape(B, Lc, SW_KV_HEADS, grp, SW_HD)
        s = jnp.einsum('bqhgd,bchd->bhgqc', qcg, kc).astype(jnp.float32) * scale
        s_sink_c = jnp.broadcast_to(sink_f[None, :, :, None, None], s.shape[:-1] + (1,))
        pc = jax.nn.softmax(jnp.concatenate([s, s_sink_c], axis=-1), axis=-1)[..., :Lc]
        out_c = jnp.einsum('bhgqc,bchd->bqhgd', pc.astype(vc.dtype), vc).reshape(B, Lc, SW_HEADS, SW_HD)
    return out_l, out_c


def s5_discretise(lam_re, lam_im, log_dt, b_re, b_im):
    lam = lax.complex(jnp.minimum(lam_re.astype(jnp.float32), SSM_RE_MAX), lam_im.astype(jnp.float32))
    dt = jnp.exp(log_dt.astype(jnp.float32))[:, None]
    lam_dt = lam * dt
    lam_bar = jnp.exp(lam_dt)
    b_bar = ((lam_bar - 1.0) / lam)[:, :, None] * lax.complex(b_re.astype(jnp.float32), b_im.astype(jnp.float32))
    return lam_dt, lam_bar, b_bar


def diag_scan(lam_bar, bu):
    a = jnp.broadcast_to(lam_bar, (1, bu.shape[1]) + lam_bar.shape)

    def combine(e1, e2):
        a1, b1 = e1
        a2, b2 = e2
        return a2 * a1, a2 * b1 + b2

    return lax.associative_scan(combine, (a, bu), axis=1)[1]


def s5_glu(y, w_glu):
    g = jax.nn.gelu(y)
    return g * jax.nn.sigmoid(g @ w_glu.astype(jnp.float32))


def s5_mixer(ul_in, uc_in, lam_re, lam_im, log_dt, b_re, b_im, c_re, c_im, d_skip, w_glu, with_ctx_out):
    B, L, _ = ul_in.shape
    Lc = uc_in.shape[1]
    gshape = (SSM_GROUPS, SSM_GROUP_CH)
    ul = ul_in.astype(jnp.float32).reshape((B, L) + gshape)
    uc = uc_in.astype(jnp.float32).reshape((B, Lc) + gshape)
    dsk = d_skip.astype(jnp.float32).reshape(gshape)
    y_l = ul * dsk
    y_c = uc * dsk if with_ctx_out else None
    steps = jnp.arange(1, L + 1, dtype=jnp.float32)[:, None, None]
    for di in range(2):
        rev = di == 1
        lam_dt, lam_bar, b_bar = s5_discretise(lam_re[di], lam_im[di], log_dt[di], b_re[di], b_im[di])
        cmat = lax.complex(c_re[di].astype(jnp.float32), c_im[di].astype(jnp.float32))
        uc_d = uc[:, ::-1] if rev else uc
        ul_d = ul[:, ::-1] if rev else ul
        h_c = diag_scan(lam_bar, jnp.einsum('gnp,btgp->btgn', b_bar, uc_d))
        h_l = diag_scan(lam_bar, jnp.einsum('gnp,btgp->btgn', b_bar, ul_d))
        h_l = h_l + jnp.exp(lam_dt[None] * steps)[None] * h_c[:, -1][:, None]
        yl = jnp.real(jnp.einsum('gpn,btgn->btgp', cmat, h_l))
        y_l = y_l + (yl[:, ::-1] if rev else yl)
        if with_ctx_out:
            yc = jnp.real(jnp.einsum('gpn,btgn->btgp', cmat, h_c))
            y_c = y_c + (yc[:, ::-1] if rev else yc)
    out_l = s5_glu(y_l.reshape(B, L, SSM_WIDTH), w_glu).astype(ul_in.dtype)
    out_c = s5_glu(y_c.reshape(B, Lc, SSM_WIDTH), w_glu).astype(uc_in.dtype) if with_ctx_out else None
    return out_l, out_c


def token_mixers(hl, hc, w_in, lam, lambda_init, subln_g, sink,
                 lam_re, lam_im, log_dt, b_re, b_im, c_re, c_im, d_skip, w_glu,
                 rope_da, rope_sw, with_ctx_out):
    B, L, _ = hl.shape
    Lc = hc.shape[1]
    pl = hl @ w_in
    pc = hc @ w_in
    o_ss = DA_COLS + SW_COLS

    def da_split(p, T):
        q = p[..., :DA_WIDTH].reshape(B, T, 2 * DA_HEADS, DA_QK)
        k = p[..., DA_WIDTH:2 * DA_WIDTH].reshape(B, T, 2 * DA_HEADS, DA_QK)
        v = p[..., 2 * DA_WIDTH:DA_COLS].reshape(B, T, DA_HEADS, DA_V)
        return q, k, v

    def to_maps(t, T):
        return t.reshape(B, T, DA_HEADS, 2, DA_QK)

    ql, kl, vl = da_split(pl, L)
    qc, kc, vc = da_split(pc, Lc)
    ql = apply_axial_rope(ql, *rope_da)
    kl = apply_axial_rope(kl, *rope_da)
    da_l, da_c = diff_attention(to_maps(ql, L), to_maps(kl, L), vl,
                                to_maps(qc, Lc), to_maps(kc, Lc), vc, lam, with_ctx_out)
    da_l = (rms_norm(da_l, subln_g) * (1.0 - lambda_init)).reshape(B, L, DA_WIDTH)

    def sw_split(p, T):
        base = DA_COLS
        q = p[..., base:base + SW_WIDTH].reshape(B, T, SW_HEADS, SW_HD)
        k = p[..., base + SW_WIDTH:base + SW_WIDTH + SW_KV_HEADS * SW_HD].reshape(B, T, SW_KV_HEADS, SW_HD)
        v = p[..., base + SW_WIDTH + SW_KV_HEADS * SW_HD:o_ss].reshape(B, T, SW_KV_HEADS, SW_HD)
        return q, k, v

    sql, skl, svl = sw_split(pl, L)
    sqc, skc, svc = sw_split(pc, Lc)
    sql = apply_axial_rope(sql, *rope_sw)
    skl = apply_axial_rope(skl, *rope_sw)
    sw_l, sw_c = window_gqa(sql, skl, svl, sqc, skc, svc, sink, with_ctx_out)
    sw_l = sw_l.reshape(B, L, SW_WIDTH)

    ss_l, ss_c = s5_mixer(pl[..., o_ss:], pc[..., o_ss:], lam_re, lam_im, log_dt,
                          b_re, b_im, c_re, c_im, d_skip, w_glu, with_ctx_out)

    ml = jnp.concatenate([da_l, sw_l, ss_l], axis=-1)
    mc = None
    if with_ctx_out:
        da_c = (rms_norm(da_c, subln_g) * (1.0 - lambda_init)).reshape(B, Lc, DA_WIDTH)
        mc = jnp.concatenate([da_c, sw_c.reshape(B, Lc, SW_WIDTH), ss_c], axis=-1)
    return ml, mc


def expert_choice_ffn(h, w_router, w_gate, w_up, w_down):
    B, n, D = h.shape
    cap = EC_CAPACITY * n // N_EXPERTS
    aff = jax.nn.softmax(jnp.einsum('bnd,de->ben', h, w_router).astype(jnp.float32), axis=1)
    g, idx = lax.top_k(aff, cap)
    xs = jax.vmap(lambda hb, ib: hb[ib])(h, idx)
    hid = jax.nn.silu(jnp.einsum('becd,edf->becf', xs, w_gate)) * jnp.einsum('becd,edf->becf', xs, w_up)
    ys = jnp.einsum('becf,efd->becd', hid, w_down) * g[..., None].astype(h.dtype)
    return jax.vmap(lambda ib, yb: jnp.zeros((n, D), yb.dtype).at[ib.reshape(-1)].add(yb.reshape(-1, D)))(idx, ys)


def setup_inputs(seed: int = 0) -> dict:
    key = jax.random.key(seed)
    ks = jax.random.split(key, 32)
    f32 = jnp.float32
    D = D_MODEL

    def nrm(k, shape, s):
        return jax.random.normal(k, shape, f32) * s

    ssm_shape = (DEPTH, 2, SSM_GROUPS, SSM_STATE)
    lam_im = jnp.broadcast_to(math.pi * jnp.arange(SSM_STATE, dtype=f32), ssm_shape) + nrm(ks[17], ssm_shape, 0.01)
    return {
        'x': nrm(ks[0], (BATCH, SEQ, D), 1.0),
        'c': nrm(ks[1], (BATCH, D), 1.0),
        'ctx': nrm(ks[2], (BATCH, CTX_LEN, D), 1.0),
        'c_ctx': nrm(ks[3], (D,), 1.0),
        'w_mod': nrm(ks[4], (DEPTH, D, N_MOD * D), D ** -0.5),
        'b_mod': nrm(ks[5], (DEPTH, N_MOD * D), 0.02),
        'norm1_g': 1.0 + nrm(ks[6], (DEPTH, D), 0.02),
        'norm2_g': 1.0 + nrm(ks[7], (DEPTH, D), 0.02),
        'w_in': nrm(ks[8], (DEPTH, D, IN_COLS), D ** -0.5),
        'w_out': nrm(ks[9], (DEPTH, MIX_WIDTH, D), MIX_WIDTH ** -0.5),
        'da_lam_q1': nrm(ks[10], (DEPTH, DA_QK), 0.1),
        'da_lam_k1': nrm(ks[11], (DEPTH, DA_QK), 0.1),
        'da_lam_q2': nrm(ks[12], (DEPTH, DA_QK), 0.1),
        'da_lam_k2': nrm(ks[13], (DEPTH, DA_QK), 0.1),
        'da_subln_g': 1.0 + nrm(ks[14], (DEPTH, DA_V), 0.02),
        'sw_sink': nrm(ks[15], (DEPTH, SW_HEADS), 0.5),
        'ssm_lam_re': -0.5 + nrm(ks[16], ssm_shape, 0.01),
        'ssm_lam_im': lam_im,
        'ssm_log_dt': jax.random.uniform(ks[18], (DEPTH, 2, SSM_GROUPS), f32,
                                         minval=math.log(SSM_DT_MIN), maxval=math.log(SSM_DT_MAX)),
        'ssm_b_re': nrm(ks[19], (DEPTH, 2, SSM_GROUPS, SSM_STATE, SSM_GROUP_CH), (2 * SSM_GROUP_CH) ** -0.5),
        'ssm_b_im': nrm(ks[20], (DEPTH, 2, SSM_GROUPS, SSM_STATE, SSM_GROUP_CH), (2 * SSM_GROUP_CH) ** -0.5),
        'ssm_c_re': nrm(ks[21], (DEPTH, 2, SSM_GROUPS, SSM_GROUP_CH, SSM_STATE), (2 * SSM_STATE) ** -0.5),
        'ssm_c_im': nrm(ks[22], (DEPTH, 2, SSM_GROUPS, SSM_GROUP_CH, SSM_STATE), (2 * SSM_STATE) ** -0.5),
        'ssm_d': nrm(ks[23], (DEPTH, SSM_WIDTH), 1.0),
        'ssm_w_glu': nrm(ks[24], (DEPTH, SSM_WIDTH, SSM_WIDTH), SSM_WIDTH ** -0.5),
        'w_router': nrm(ks[25], (DEPTH, D, N_EXPERTS), D ** -0.5),
        'w_gate': nrm(ks[26], (DEPTH, N_EXPERTS, D, EXPERT_FF), D ** -0.5),
        'w_up': nrm(ks[27], (DEPTH, N_EXPERTS, D, EXPERT_FF), D ** -0.5),
        'w_down': nrm(ks[28], (DEPTH, N_EXPERTS, EXPERT_FF, D), EXPERT_FF ** -0.5),
        'final_g': 1.0 + nrm(ks[29], (D,), 0.02),
    }


def reference(x, c, ctx, c_ctx, w_mod, b_mod, norm1_g, norm2_g, w_in, w_out,
              da_lam_q1, da_lam_k1, da_lam_q2, da_lam_k2, da_subln_g, sw_sink,
              ssm_lam_re, ssm_lam_im, ssm_log_dt, ssm_b_re, ssm_b_im, ssm_c_re, ssm_c_im,
              ssm_d, ssm_w_glu, w_router, w_gate, w_up, w_down, final_g):
    L = x.shape[1]
    rows = L // GRID_W
    rope_da = axial_rope_tables(rows, DA_QK)
    rope_sw = axial_rope_tables(rows, SW_HD)
    silu_c = jax.nn.silu(c)[:, None, :]
    silu_cc = jax.nn.silu(c_ctx)[None, None, :]
    xl, xc = x, ctx
    for l in range(DEPTH):
        last = l == DEPTH - 1
        lambda_init = 0.8 - 0.6 * math.exp(-0.3 * l)
        lam = (jnp.exp(jnp.sum(da_lam_q1[l].astype(jnp.float32) * da_lam_k1[l].astype(jnp.float32)))
               - jnp.exp(jnp.sum(da_lam_q2[l].astype(jnp.float32) * da_lam_k2[l].astype(jnp.float32)))
               + lambda_init)
        mods_l = jnp.split(silu_c @ w_mod[l] + b_mod[l], N_MOD, axis=-1)
        mods_c = jnp.split(silu_cc @ w_mod[l] + b_mod[l], N_MOD, axis=-1)
        hl = modulate(rms_norm(xl, norm1_g[l]), mods_l[0], mods_l[1])
        hc = modulate(rms_norm(xc, norm1_g[l]), mods_c[0], mods_c[1])
        ml, mc = token_mixers(hl, hc, w_in[l], lam, lambda_init, da_subln_g[l], sw_sink[l],
                              ssm_lam_re[l], ssm_lam_im[l], ssm_log_dt[l], ssm_b_re[l], ssm_b_im[l],
                              ssm_c_re[l], ssm_c_im[l], ssm_d[l], ssm_w_glu[l],
                              rope_da, rope_sw, not last)
        xl = xl + mods_l[2] * (ml @ w_out[l])
        hl = modulate(rms_norm(xl, norm2_g[l]), mods_l[3], mods_l[4])
        xl = xl + mods_l[5] * expert_choice_ffn(hl, w_router[l], w_gate[l], w_up[l], w_down[l])
        if not last:
            xc = xc + mods_c[2] * (mc @ w_out[l])
            hc = modulate(rms_norm(xc, norm2_g[l]), mods_c[3], mods_c[4])
            xc = xc + mods_c[5] * expert_choice_ffn(hc, w_router[l], w_gate[l], w_up[l], w_down[l])
    return rms_norm(xl, final_g)
```

```python
import functools
import math

import jax
import jax.numpy as jnp
from jax import lax
from jax.experimental import pallas as pl
from jax.experimental.pallas import tpu as pltpu

F32 = jnp.float32
BF16 = jnp.bfloat16

GRID_W = 64
ROPE_THETA = 10000.0
NORM_EPS = 1e-6
NEG_INF = -1e30

DA_HEADS = 6
DA_QK = 64
DA_V = 128
DA_WIDTH = 768
SW_HEADS = 6
SW_KV_HEADS = 2
SW_GROUP = SW_HEADS // SW_KV_HEADS
SW_HD = 128
SW_WINDOW = 128
SW_WIDTH = 768
SSM_GROUPS = 32
SSM_GROUP_CH = 16
SSM_STATE = 64
SSM_WIDTH = 512
SSM_LANES = SSM_GROUPS * SSM_STATE
SSM_RE_MAX = -1e-4
ATT_COLS = 3 * DA_WIDTH + SW_WIDTH + 2 * SW_KV_HEADS * SW_HD
N_EXPERTS = 16
EC_CAPACITY = 2
N_MOD = 6

TM = 512
TQ = 256
TT = 32
LANE = 128
VMEM_LIMIT = 56 * 1024 * 1024


def _cparams(sem):
    return pltpu.CompilerParams(dimension_semantics=sem, vmem_limit_bytes=VMEM_LIMIT)


def _mods_kernel(c_ref, w_ref, b_ref, o_ref):
    c = c_ref[...]
    s = (c * jax.nn.sigmoid(c)).astype(BF16)
    o_ref[...] = jnp.dot(s, w_ref[...].astype(BF16), preferred_element_type=F32) + b_ref[...]


def _mods(craw, w_mod, b_mod):
    depth, d, n = w_mod.shape
    tn = 1024
    return pl.pallas_call(
        _mods_kernel,
        grid=(depth, n // tn),
        in_specs=[
            pl.BlockSpec((16, d), lambda l, j: (0, 0)),
            pl.BlockSpec((None, d, tn), lambda l, j: (l, 0, j)),
            pl.BlockSpec((None, 1, tn), lambda l, j: (l, 0, j)),
        ],
        out_specs=pl.BlockSpec((None, 16, tn), lambda l, j: (l, 0, j)),
        out_shape=jax.ShapeDtypeStruct((depth, 16, n), F32),
        compiler_params=_cparams(("parallel", "parallel")),
        name="mods",
    )(craw, w_mod, b_mod.reshape(depth, 1, n))


def _rope(acc, cos, sa, sb, quarter):
    return (acc * cos + pltpu.roll(acc, LANE - quarter, 1) * sa + pltpu.roll(acc, quarter, 1) * sb)


def _k1_kernel(x_ref, sh_ref, sc_ref, g_ref, w_ref, rope_ref, p_ref, u_ref):
    x = x_ref[...]
    y = x * lax.rsqrt(jnp.mean(x * x, axis=-1, keepdims=True) + NORM_EPS) * g_ref[...]
    h = (y * (1.0 + sc_ref[...]) + sh_ref[...]).astype(BF16)
    n_chunks = (ATT_COLS + SSM_WIDTH) // 256
    for j in range(n_chunks):
        c0 = j * 256
        acc = jnp.dot(h, w_ref[:, c0:c0 + 256], preferred_element_type=F32)
        if c0 >= ATT_COLS:
            u_ref[:, c0 - ATT_COLS:c0 - ATT_COLS + 256] = acc
            continue
        if c0 < 2 * DA_WIDTH:
            t0, quarter = 0, DA_QK // 4
        elif 3 * DA_WIDTH <= c0 < 3 * DA_WIDTH + SW_WIDTH + SW_KV_HEADS * SW_HD:
            t0, quarter = 3 * LANE, SW_HD // 4
        else:
            t0 = None
        if t0 is not None:
            cos = rope_ref[:, t0:t0 + LANE]
            sa = rope_ref[:, t0 + LANE:t0 + 2 * LANE]
            sb = rope_ref[:, t0 + 2 * LANE:t0 + 3 * LANE]
            acc = jnp.concatenate(
                [_rope(acc[:, :LANE], cos, sa, sb, quarter), _rope(acc[:, LANE:], cos, sa, sb, quarter)], axis=1)
        p_ref[:, c0:c0 + 256] = acc.astype(BF16)


def _k1(xall, mods4, g1, w_in, rope, n_lat_tiles):
    b, s_pad, d = xall.shape
    ncols = w_in.shape[1]

    def mrow(k):
        return lambda bi, i: (jnp.where(i >= n_lat_tiles, 8, bi), k, 0, 0)

    return pl.pallas_call(
        _k1_kernel,
        grid=(b, s_pad // TM),
        in_specs=[
            pl.BlockSpec((None, TM, d), lambda bi, i: (bi, i, 0)),
            pl.BlockSpec((None, None, 1, d), mrow(0)),
            pl.BlockSpec((None, None, 1, d), mrow(1)),
            pl.BlockSpec((1, d), lambda bi, i: (0, 0)),
            pl.BlockSpec((d, ncols), lambda bi, i: (0, 0), pipeline_mode=pl.Buffered(1)),
            pl.BlockSpec((TM, 6 * LANE), lambda bi, i: (i, 0)),
        ],
        out_specs=[
            pl.BlockSpec((None, TM, ATT_COLS), lambda bi, i: (bi, i, 0)),
            pl.BlockSpec((TM, SSM_WIDTH), lambda bi, i: (i, bi)),
        ],
        out_shape=[
            jax.ShapeDtypeStruct((b, s_pad, ATT_COLS), BF16),
            jax.ShapeDtypeStruct((s_pad, b * SSM_WIDTH), F32),
        ],
        compiler_params=_cparams(("parallel", "parallel")),
        name="norm_inproj",
    )(xall, mods4, mods4, g1, w_in, rope)


def _softmax_parts(s):
    m = jnp.max(s, axis=-1, keepdims=True)
    e = jnp.exp(s - m)
    return e, jnp.sum(e, axis=-1, keepdims=True)


def _da_kernel(q_ref, k_ref, v_ref, lp_ref, g_ref, o_ref, *, n_lat_q, l_lat, l_ctx, lambda_init):
    i = pl.program_id(2)
    lp = lp_ref[...]
    lam = (jnp.exp(jnp.sum(lp[0:1] * lp[1:2], axis=-1, keepdims=True))
           - jnp.exp(jnp.sum(lp[2:3] * lp[3:4], axis=-1, keepdims=True)) + lambda_init)
    scale = DA_QK ** -0.5

    def attend(k, v):
        q = q_ref[...]
        lane = lax.broadcasted_iota(jnp.int32, q.shape, 1)
        zero = jnp.zeros_like(q)
        dn = (((1,), (1,)), ((), ()))
        s1 = lax.dot_general(jnp.where(lane < DA_QK, q, zero), k, dn, preferred_element_type=F32) * scale
        s2 = lax.dot_general(jnp.where(lane >= DA_QK, q, zero), k, dn, preferred_element_type=F32) * scale
        e1, l1 = _softmax_parts(s1)
        e2, l2 = _softmax_parts(s2)
        w = e1 * (1.0 / l1) - e2 * (lam / l2)
        o = jnp.dot(w.astype(BF16), v, preferred_element_type=F32)
        o = o * lax.rsqrt(jnp.mean(o * o, axis=-1, keepdims=True) + NORM_EPS) * g_ref[...]
        o_ref[...] = (o * (1.0 - lambda_init)).astype(BF16)

    @pl.when(i < n_lat_q)
    def _():
        attend(k_ref[0:l_lat + l_ctx, :], v_ref[0:l_lat + l_ctx, :])

    @pl.when(i == n_lat_q)
    def _():
        attend(k_ref[l_lat:l_lat + l_ctx, :], v_ref[l_lat:l_lat + l_ctx, :])

    @pl.when(i > n_lat_q)
    def _():
        o_ref[...] = jnp.zeros_like(o_ref)


def _diff_attention(p, lam_params, subln_g, l_lat, l_ctx, lambda_init):
    b, s_pad, _ = p.shape
    kern = functools.partial(_da_kernel, n_lat_q=l_lat // TQ, l_lat=l_lat, l_ctx=l_ctx, lambda_init=lambda_init)
    return pl.pallas_call(
        kern,
        grid=(b, DA_HEADS, s_pad // TQ),
        in_specs=[
            pl.BlockSpec((None, TQ, LANE), lambda bi, h, i: (bi, i, h)),
            pl.BlockSpec((None, s_pad, LANE), lambda bi, h, i: (bi, 0, DA_HEADS + h)),
            pl.BlockSpec((None, s_pad, LANE), lambda bi, h, i: (bi, 0, 2 * DA_HEADS + h)),
            pl.BlockSpec((8, LANE), lambda bi, h, i: (0, 0)),
            pl.BlockSpec((1, LANE), lambda bi, h, i: (0, 0)),
        ],
        out_specs=pl.BlockSpec((None, TQ, LANE), lambda bi, h, i: (bi, i, h)),
        out_shape=jax.ShapeDtypeStruct((b, s_pad, DA_WIDTH), BF16),
        compiler_params=_cparams(("parallel", "parallel", "arbitrary")),
        name="diff_attention",
    )(p, p, p, lam_params, subln_g)


def _sw_kernel(sink_ref, q_ref, k_ref, v_ref, o_ref, *, n_lat_q, l_lat, l_ctx):
    i = pl.program_id(2)
    kvh = pl.program_id(1)
    scale = SW_HD ** -0.5
    dn = (((1,), (1,)), ((), ()))
    band_w = TQ + 2 * SW_WINDOW
    kc = k_ref[l_lat:l_lat + l_ctx, :]
    vc = v_ref[l_lat:l_lat + l_ctx, :]

    def ctx_scores(qg):
        return lax.dot_general(qg, kc, dn, preferred_element_type=F32) * scale

    @pl.when(i < n_lat_q)
    def _():
        q0 = i * TQ
        kstart = pl.multiple_of(jnp.clip(q0 - SW_WINDOW, 0, l_lat - band_w), SW_WINDOW)
        kb = k_ref[pl.ds(kstart, band_w), :]
        vb = v_ref[pl.ds(kstart, band_w), :]
        qpos = q0 + lax.broadcasted_iota(jnp.int32, (TQ, band_w), 0)
        kpos = kstart + lax.broadcasted_iota(jnp.int32, (TQ, band_w), 1)
        in_band = jnp.abs(qpos - kpos) <= SW_WINDOW
        for g in range(SW_GROUP):
            qg = q_ref[:, g * SW_HD:(g + 1) * SW_HD]
            sink = sink_ref[kvh * SW_GROUP + g]
            sb = jnp.where(in_band, lax.dot_general(qg, kb, dn, preferred_element_type=F32) * scale, NEG_INF)
            sc = ctx_scores(qg)
            m = jnp.maximum(jnp.maximum(jnp.max(sb, axis=-1, keepdims=True), jnp.max(sc, axis=-1, keepdims=True)), sink)
            eb = jnp.exp(sb - m)
            ec = jnp.exp(sc - m)
            denom = jnp.sum(eb, axis=-1, keepdims=True) + jnp.sum(ec, axis=-1, keepdims=True) + jnp.exp(sink - m)
            inv = 1.0 / denom
            o = (jnp.dot((eb * inv).astype(BF16), vb, preferred_element_type=F32)
                 + jnp.dot((ec * inv).astype(BF16), vc, preferred_element_type=F32))
            o_ref[:, g * SW_HD:(g + 1) * SW_HD] = o.astype(BF16)

    @pl.when(i == n_lat_q)
    def _():
        for g in range(SW_GROUP):
            qg = q_ref[:, g * SW_HD:(g + 1) * SW_HD]
            sink = sink_ref[kvh * SW_GROUP + g]
            sc = ctx_scores(qg)
            m = jnp.maximum(jnp.max(sc, axis=-1, keepdims=True), sink)
            ec = jnp.exp(sc - m)
            inv = 1.0 / (jnp.sum(ec, axis=-1, keepdims=True) + jnp.exp(sink - m))
            o = jnp.dot((ec * inv).astype(BF16), vc, preferred_element_type=F32)
            o_ref[:, g * SW_HD:(g + 1) * SW_HD] = o.astype(BF16)

    @pl.when(i > n_lat_q)
    def _():
        o_ref[...] = jnp.zeros_like(o_ref)


def _window_attention(p, sink, l_lat, l_ctx):
    b, s_pad, _ = p.shape
    gw = SW_GROUP * SW_HD
    q_blk0 = 3 * DA_WIDTH // gw
    k_blk0 = (3 * DA_WIDTH + SW_WIDTH) // SW_HD
    v_blk0 = k_blk0 + SW_KV_HEADS
    kern = functools.partial(_sw_kernel, n_lat_q=l_lat // TQ, l_lat=l_lat, l_ctx=l_ctx)
    return pl.pallas_call(
        kern,
        grid=(b, SW_KV_HEADS, s_pad // TQ),
        in_specs=[
            pl.BlockSpec(memory_space=pltpu.SMEM),
            pl.BlockSpec((None, TQ, gw), lambda bi, h, i: (bi, i, q_blk0 + h)),
            pl.BlockSpec((None, s_pad, SW_HD), lambda bi, h, i: (bi, 0, k_blk0 + h)),
            pl.BlockSpec((None, s_pad, SW_HD), lambda bi, h, i: (bi, 0, v_blk0 + h)),
        ],
        out_specs=pl.BlockSpec((None, TQ, gw), lambda bi, h, i: (bi, i, h)),
        out_shape=jax.ShapeDtypeStruct((b, s_pad, SW_WIDTH), BF16),
        compiler_params=_cparams(("parallel", "parallel", "arbitrary")),
        name="window_attention",
    )(sink, p, p, p)


def _s5_discretise(lam_re, lam_im, log_dt, b_re, b_im):
    lr = jnp.minimum(lam_re.astype(F32), SSM_RE_MAX)
    li = lam_im.astype(F32)
    dt = jnp.exp(log_dt.astype(F32))[..., None]
    mag = jnp.exp(lr * dt)
    lbr = mag * jnp.cos(li * dt)
    lbi = mag * jnp.sin(li * dt)
    den = lr * lr + li * li
    cr = ((lbr - 1.0) * lr + lbi * li) / den
    ci = (lbi * lr - (lbr - 1.0) * li) / den
    br = b_re.astype(F32)
    bi = b_im.astype(F32)
    bbr = cr[..., None] * br - ci[..., None] * bi
    bbi = cr[..., None] * bi + ci[..., None] * br
    return lbr, lbi, bbr, bbi


def _s5_matrices(lam_re, lam_im, log_dt, b_re, b_im, c_re, c_im):
    lbr, lbi, bbr, bbi = _s5_discretise(lam_re, lam_im, log_dt, b_re, b_im)
    eye = jnp.eye(SSM_GROUPS, dtype=F32)

    def in_map(bb):
        return jnp.einsum("dgnp,gh->dgphn", bb, eye).reshape(2, SSM_WIDTH, SSM_LANES)

    def out_map(cc):
        return jnp.einsum("dgpn,gh->dgnhp", cc, eye).reshape(2, SSM_LANES, SSM_WIDTH)

    b_blk = jnp.concatenate([in_map(bbr), in_map(bbi)], axis=2).astype(BF16)
    c_blk = jnp.concatenate([out_map(c_re.astype(F32)), out_map(-c_im.astype(F32))], axis=1).astype(BF16)
    lam_b = jnp.concatenate([lbr.reshape(2, 1, SSM_LANES), lbi.reshape(2, 1, SSM_LANES)], axis=2)
    lam_b = jnp.broadcast_to(lam_b, (2, 8, 2 * SSM_LANES))
    return b_blk, c_blk, lam_b


def _s5_kernel(u_ref, b_ref, c_ref, lam_ref, y_ref, bu_ref, hs_ref, h_ref, *, n_batch):
    d = pl.program_id(0)
    s = pl.program_id(1)
    rows = TT * n_batch

    @pl.when(s == 0)
    def _():
        h_ref[...] = jnp.zeros_like(h_ref)

    u = u_ref[...].astype(BF16)
    for c in range(2 * SSM_LANES // 256):
        k0 = ((c % (SSM_LANES // 256)) * 256 // SSM_STATE * SSM_GROUP_CH) // LANE * LANE
        bu_ref[:, c * 256:(c + 1) * 256] = jnp.dot(
            u[:, k0:k0 + LANE], b_ref[k0:k0 + LANE, c * 256:(c + 1) * 256], preferred_element_type=F32)

    cw = 512
    for c in range(SSM_LANES // cw):
        re = slice(c * cw, (c + 1) * cw)
        im = slice(SSM_LANES + c * cw, SSM_LANES + (c + 1) * cw)
        lr = lam_ref[:, re]
        li = lam_ref[:, im]

        def step(t, carry, re=re, im=im, lr=lr, li=li):
            hr, hi = carry
            tt = jnp.where(d == 0, t, TT - 1 - t)
            r = pl.multiple_of(tt * n_batch, n_batch)
            nr = lr * hr - li * hi + bu_ref[pl.ds(r, n_batch), re]
            ni = lr * hi + li * hr + bu_ref[pl.ds(r, n_batch), im]
            hs_ref[pl.ds(r, n_batch), re] = nr
            hs_ref[pl.ds(r, n_batch), im] = ni
            return nr, ni

        hr, hi = lax.fori_loop(0, TT, step, (h_ref[:, re], h_ref[:, im]), unroll=4)
        h_ref[:, re] = hr
        h_ref[:, im] = hi

    half = SSM_LANES // 2
    for j in range(SSM_WIDTH // 256):
        acc = jnp.dot(hs_ref[:, j * half:(j + 1) * half].astype(BF16),
                      c_ref[j * half:(j + 1) * half, j * 256:(j + 1) * 256], preferred_element_type=F32)
        acc += jnp.dot(hs_ref[:, SSM_LANES + j * half:SSM_LANES + (j + 1) * half].astype(BF16),
                       c_ref[SSM_LANES + j * half:SSM_LANES + (j + 1) * half, j * 256:(j + 1) * 256],
                       preferred_element_type=F32)
        y_ref[:, j * 256:(j + 1) * 256] = acc


def _s5_scan(u_rows, b_blk, c_blk, lam_b, n_batch, l_lat, l_ctx):
    rows = TT * n_batch
    n_lat, n_ctx = l_lat // TT, l_ctx // TT
    n_steps = n_lat + n_ctx

    def tile(d, s):
        fwd = jnp.where(s < n_ctx, n_lat + s, s - n_ctx)
        bwd = jnp.where(s < n_ctx, n_lat + n_ctx - 1 - s, n_lat - 1 - (s - n_ctx))
        return jnp.where(d == 0, fwd, bwd)

    kern = functools.partial(_s5_kernel, n_batch=n_batch)
    return pl.pallas_call(
        kern,
        grid=(2, n_steps),
        in_specs=[
            pl.BlockSpec((rows, SSM_WIDTH), lambda d, s: (tile(d, s), 0)),
            pl.BlockSpec((None, SSM_WIDTH, 2 * SSM_LANES), lambda d, s: (d, 0, 0)),
            pl.BlockSpec((None, 2 * SSM_LANES, SSM_WIDTH), lambda d, s: (d, 0, 0)),
            pl.BlockSpec((None, 8, 2 * SSM_LANES), lambda d, s: (d, 0, 0)),
        ],
        out_specs=pl.BlockSpec((None, rows, SSM_WIDTH), lambda d, s: (d, tile(d, s), 0)),
        out_shape=jax.ShapeDtypeStruct((2, (l_lat + l_ctx) * n_batch, SSM_WIDTH), F32),
        scratch_shapes=[
            pltpu.VMEM((rows, 2 * SSM_LANES), F32),
            pltpu.VMEM((rows, 2 * SSM_LANES), F32),
            pltpu.VMEM((n_batch, 2 * SSM_LANES), F32),
        ],
        compiler_params=_cparams(("arbitrary", "arbitrary")),
        name="s5_scan",
    )(u_rows, b_blk, c_blk, lam_b)


def _glu_kernel(u_ref, y_ref, d_ref, w_ref, o_ref, *, n_valid_tiles):
    @pl.when(pl.program_id(0) < n_valid_tiles)
    def _():
        y = u_ref[...] * d_ref[...] + y_ref[0] + y_ref[1]
        g = jax.nn.gelu(y)
        z = jnp.dot(g.astype(BF16), w_ref[...], preferred_element_type=F32)
        o_ref[...] = (g * jax.nn.sigmoid(z)).astype(BF16)

    @pl.when(pl.program_id(0) >= n_valid_tiles)
    def _():
        o_ref[...] = jnp.zeros_like(o_ref)


def _s5_glu(u_rows, y, d_skip, w_glu, n_valid_rows):
    n_rows = u_rows.shape[0]
    tr = 512
    n_valid_tiles = n_valid_rows // tr
    return pl.pallas_call(
        functools.partial(_glu_kernel, n_valid_tiles=n_valid_tiles),
        grid=(n_rows // tr,),
        in_specs=[
            pl.BlockSpec((tr, SSM_WIDTH), lambda i: (i, 0)),
            pl.BlockSpec((2, tr, SSM_WIDTH), lambda i: (0, jnp.minimum(i, n_valid_tiles - 1), 0)),
            pl.BlockSpec((1, SSM_WIDTH), lambda i: (0, 0)),
            pl.BlockSpec((SSM_WIDTH, SSM_WIDTH), lambda i: (0, 0)),
        ],
        out_specs=pl.BlockSpec((tr, SSM_WIDTH), lambda i: (i, 0)),
        out_shape=jax.ShapeDtypeStruct((n_rows, SSM_WIDTH), BF16),
        compiler_params=_cparams(("parallel",)),
        name="s5_glu",
    )(u_rows, y, d_skip, w_glu)


def _kout_kernel(da_ref, sw_ref, ss_ref, x_ref, gate_ref, sh_ref, sc_ref, g_ref, w_ref, wr_ref,
                 x1_ref, h2_ref, lg_ref):
    acc = jnp.dot(da_ref[...], w_ref[0:DA_WIDTH, :], preferred_element_type=F32)
    acc += jnp.dot(sw_ref[...], w_ref[DA_WIDTH:DA_WIDTH + SW_WIDTH, :], preferred_element_type=F32)
    acc += jnp.dot(ss_ref[...], w_ref[DA_WIDTH + SW_WIDTH:, :], preferred_element_type=F32)
    x1 = x_ref[...] + gate_ref[...] * acc
    x1_ref[...] = x1
    y = x1 * lax.rsqrt(jnp.mean(x1 * x1, axis=-1, keepdims=True) + NORM_EPS) * g_ref[...]
    h2 = y * (1.0 + sc_ref[...]) + sh_ref[...]
    h2_ref[...] = h2
    lg_ref[...] = jnp.dot(h2.astype(BF16), wr_ref[...], preferred_element_type=F32)


def _kout(da, sw, ss, xall, mods4, g2, w_out, w_router, n_lat_tiles):
    b, s_pad, d = xall.shape

    def mrow(k):
        return lambda bi, i: (jnp.where(i >= n_lat_tiles, 8, bi), k, 0, 0)

    tok = lambda bi, i: (bi, i, 0)
    return pl.pallas_call(
        _kout_kernel,
        grid=(b, s_pad // TM),
        in_specs=[
            pl.BlockSpec((None, TM, DA_WIDTH), tok),
            pl.BlockSpec((None, TM, SW_WIDTH), tok),
            pl.BlockSpec((TM, SSM_WIDTH), lambda bi, i: (i, bi)),
            pl.BlockSpec((None, TM, d), tok),
            pl.BlockSpec((None, None, 1, d), mrow(2)),
            pl.BlockSpec((None, None, 1, d), mrow(3)),
            pl.BlockSpec((None, None, 1, d), mrow(4)),
            pl.BlockSpec((1, d), lambda bi, i: (0, 0)),
            pl.BlockSpec((d, d), lambda bi, i: (0, 0), pipeline_mode=pl.Buffered(1)),
            pl.BlockSpec((d, LANE), lambda bi, i: (0, 0)),
        ],
        out_specs=[
            pl.BlockSpec((None, TM, d), tok),
            pl.BlockSpec((None, TM, d), tok),
            pl.BlockSpec((None, TM, LANE), tok),
        ],
        out_shape=[
            jax.ShapeDtypeStruct((b, s_pad, d), F32),
            jax.ShapeDtypeStruct((b, s_pad, d), F32),
            jax.ShapeDtypeStruct((b, s_pad, LANE), F32),
        ],
        compiler_params=_cparams(("parallel", "parallel")),
        name="outproj_norm_router",
    )(da, sw, ss, xall, mods4, mods4, mods4, g2, w_out, w_router)


def _lane_cumsum(x, tri):
    n = x.shape[1]
    total = jnp.zeros((x.shape[0], 1), F32)
    parts = []
    for c in range(n // 256):
        loc = jnp.dot(x[:, c * 256:(c + 1) * 256].astype(BF16), tri, preferred_element_type=F32)
        parts.append(loc + total)
        total = total + loc[:, 255:256]
    return jnp.concatenate(parts, axis=1)


def _route_kernel(lg_ref, tok_ref, idx_ref, gate_ref, *, n_tok, cap):
    lg = lg_ref[...]
    lane = lax.broadcasted_iota(jnp.int32, lg.shape, 1)
    lg = jnp.where(lane < N_EXPERTS, lg, NEG_INF)
    e = jnp.exp(lg - jnp.max(lg, axis=-1, keepdims=True))
    aff = e / jnp.sum(e, axis=-1, keepdims=True)
    aff_t = jnp.transpose(aff)[0:N_EXPERTS, :]
    bits = pltpu.bitcast(aff_t, jnp.int32)

    def search(k, thr):
        cand = thr | lax.shift_left(jnp.int32(1), 30 - k)
        cnt = jnp.sum((bits >= cand).astype(F32), axis=-1, keepdims=True)
        return jnp.where(cnt >= cap, cand, thr)

    thr = lax.fori_loop(0, 31, search, jnp.zeros((N_EXPERTS, 1), jnp.int32))
    gt = bits > thr
    eq = bits == thr
    r_i = lax.broadcasted_iota(jnp.int32, (256, 256), 0)
    c_i = lax.broadcasted_iota(jnp.int32, (256, 256), 1)
    tri = (r_i <= c_i).astype(BF16)
    need = cap - jnp.sum(gt.astype(F32), axis=-1, keepdims=True)
    eq_f = eq.astype(F32)
    eq_rank = _lane_cumsum(eq_f, tri) - eq_f
    sel = jnp.where(gt, 1.0, jnp.where(eq & (eq_rank < need), 1.0, 0.0))
    pos = _lane_cumsum(sel, tri) - sel
    pos = jnp.where(sel > 0.0, pos, -1.0)

    a_hi = aff.astype(BF16)
    r1 = aff - a_hi.astype(F32)
    a_mid = r1.astype(BF16)
    a_lo = (r1 - a_mid.astype(F32)).astype(BF16)
    tok = tok_ref[...]
    payload = jnp.where(lane < 16, a_hi.astype(F32),
                        jnp.where(lane < 32, pltpu.roll(a_mid.astype(F32), 16, 1),
                                  jnp.where(lane < 48, pltpu.roll(a_lo.astype(F32), 32, 1), tok))).astype(BF16)

    slot = lax.broadcasted_iota(jnp.int32, (cap, 256), 0).astype(F32)
    out_lane = lax.broadcasted_iota(jnp.int32, (cap, LANE), 1)
    for ex in range(N_EXPERTS):
        acc = jnp.zeros((cap, LANE), F32)
        for c in range(n_tok // 256):
            onehot = jnp.where(pos[ex:ex + 1, c * 256:(c + 1) * 256] == slot, 1.0, 0.0).astype(BF16)
            acc += jnp.dot(onehot, payload[c * 256:(c + 1) * 256, :], preferred_element_type=F32)
        pick = lambda l: jnp.sum(jnp.where(out_lane == l, acc, 0.0), axis=-1, keepdims=True)
        gate_ref[ex] = (pick(ex) + pick(16 + ex)) + pick(32 + ex)
        idx_ref[ex] = (pick(48) * 64.0 + pick(49)).astype(jnp.int32)


def _route(lg, tok_cols, row_block, n_tok):
    b = lg.shape[0]
    cap = EC_CAPACITY * n_tok // N_EXPERTS
    kern = functools.partial(_route_kernel, n_tok=n_tok, cap=cap)
    idx, gate = pl.pallas_call(
        kern,
        grid=(b,),
        in_specs=[
            pl.BlockSpec((None, n_tok, LANE), lambda bi: (bi, row_block, 0)),
            pl.BlockSpec((n_tok, LANE), lambda bi: (0, 0)),
        ],
        out_specs=[
            pl.BlockSpec((None, N_EXPERTS, cap, 1), lambda bi: (bi, 0, 0, 0)),
            pl.BlockSpec((None, N_EXPERTS, cap, 1), lambda bi: (bi, 0, 0, 0)),
        ],
        out_shape=[
            jax.ShapeDtypeStruct((b, N_EXPERTS, cap, 1), jnp.int32),
            jax.ShapeDtypeStruct((b, N_EXPERTS, cap, 1), F32),
        ],
        compiler_params=_cparams(("parallel",)),
        name="route",
    )(lg, tok_cols)
    return idx[..., 0], gate


def _token_columns(n_tok):
    t = jnp.arange(n_tok, dtype=jnp.int32)[:, None]
    lane = jnp.arange(LANE, dtype=jnp.int32)[None, :]
    return jnp.where(lane == 48, t // 64, jnp.where(lane == 49, t % 64, 0)).astype(F32)


def _ffn_kernel(idx_ref, gate_ref, m5_ref, m5c_ref, wg_ref, wu_ref, wd_ref, h2_hbm, acc_in_hbm, acc_hbm,
                xbuf, abuf, sem_x, sem_a, sem_o, *, n_rows, n_lat_rows):
    del acc_in_hbm
    b = pl.program_id(1)

    def x_copy(c):
        return pltpu.make_async_copy(h2_hbm.at[b, pl.ds(idx_ref[0, c], 1), :], xbuf.at[pl.ds(c, 1), :], sem_x)

    def acc_load(c):
        return pltpu.make_async_copy(acc_hbm.at[b, pl.ds(idx_ref[0, c], 1), :], abuf.at[pl.ds(c, 1), :], sem_a)

    def acc_store(c):
        return pltpu.make_async_copy(abuf.at[pl.ds(c, 1), :], acc_hbm.at[b, pl.ds(idx_ref[0, c], 1), :], sem_o)

    def gather(c, carry):
        x_copy(c).start()
        acc_load(c).start()
        return carry

    lax.fori_loop(0, n_rows, gather, 0)

    def gather_wait(c, carry):
        x_copy(c).wait()
        acc_load(c).wait()
        return carry

    lax.fori_loop(0, n_rows, gather_wait, 0)

    x = xbuf[...].astype(BF16)
    hg = jnp.dot(x, wg_ref[...], preferred_element_type=F32)
    hu = jnp.dot(x, wu_ref[...], preferred_element_type=F32)
    hid = (hg * jax.nn.sigmoid(hg) * hu).astype(BF16)
    y = jnp.dot(hid, wd_ref[...], preferred_element_type=F32)
    row = lax.broadcasted_iota(jnp.int32, (n_rows, 1), 0)
    m5 = jnp.where(row < n_lat_rows, m5_ref[...], m5c_ref[...])
    abuf[...] = abuf[...] + m5 * (y * gate_ref[...])

    def scatter(c, carry):
        acc_store(c).start()
        return carry

    lax.fori_loop(0, n_rows, scatter, 0)

    def scatter_wait(c, carry):
        acc_store(c).wait()
        return carry

    lax.fori_loop(0, n_rows, scatter_wait, 0)


def _expert_ffn(idx, gate, mods4, w_gate, w_up, w_down, h2, x1, n_lat_rows):
    b, s_pad, d = x1.shape
    n_rows = idx.shape[2]
    ff = w_gate.shape[2]
    kern = functools.partial(_ffn_kernel, n_rows=n_rows, n_lat_rows=n_lat_rows)
    return pl.pallas_call(
        kern,
        grid=(N_EXPERTS, b),
        in_specs=[
            pl.BlockSpec((None, None, 1, n_rows), lambda e, bi: (bi, e, 0, 0), memory_space=pltpu.SMEM),
            pl.BlockSpec((None, None, n_rows, 1), lambda e, bi: (bi, e, 0, 0)),
            pl.BlockSpec((None, None, 1, d), lambda e, bi: (bi, 5, 0, 0)),
            pl.BlockSpec((None, None, 1, d), lambda e, bi: (8, 5, 0, 0)),
            pl.BlockSpec((None, d, ff), lambda e, bi: (e, 0, 0)),
            pl.BlockSpec((None, d, ff), lambda e, bi: (e, 0, 0)),
            pl.BlockSpec((None, ff, d), lambda e, bi: (e, 0, 0)),
            pl.BlockSpec(memory_space=pl.ANY),
            pl.BlockSpec(memory_space=pl.ANY),
        ],
        out_specs=pl.BlockSpec(memory_space=pl.ANY),
        out_shape=jax.ShapeDtypeStruct((b, s_pad, d), F32),
        scratch_shapes=[
            pltpu.VMEM((n_rows, d), F32),
            pltpu.VMEM((n_rows, d), F32),
            pltpu.SemaphoreType.DMA(()),
            pltpu.SemaphoreType.DMA(()),
            pltpu.SemaphoreType.DMA(()),
        ],
        input_output_aliases={8: 0},
        compiler_params=pltpu.CompilerParams(
            dimension_semantics=("arbitrary", "arbitrary"), vmem_limit_bytes=VMEM_LIMIT, has_side_effects=True),
        name="expert_ffn",
    )(idx[:, :, None, :], gate, mods4, mods4, w_gate, w_up, w_down, h2, x1)


def _final_kernel(x_ref, g_ref, o_ref):
    x = x_ref[...]
    o_ref[...] = x * lax.rsqrt(jnp.mean(x * x, axis=-1, keepdims=True) + NORM_EPS) * g_ref[...]


def _final_norm(xall, g, l_lat):
    b, _, d = xall.shape
    return pl.pallas_call(
        _final_kernel,
        grid=(b, l_lat // TM),
        in_specs=[pl.BlockSpec((None, TM, d), lambda bi, i: (bi, i, 0)), pl.BlockSpec((1, d), lambda bi, i: (0, 0))],
        out_specs=pl.BlockSpec((None, TM, d), lambda bi, i: (bi, i, 0)),
        out_shape=jax.ShapeDtypeStruct((b, l_lat, d), F32),
        compiler_params=_cparams(("parallel", "parallel")),
        name="final_norm",
    )(xall, g)


def _rope_table(l_lat, s_pad, head_dim):
    q = head_dim // 4
    inv_freq = ROPE_THETA ** (-jnp.arange(q, dtype=F32) / q)
    pos = jnp.arange(l_lat)
    ang_r = (pos // GRID_W).astype(F32)[:, None] * inv_freq[None, :]
    ang_c = (pos % GRID_W).astype(F32)[:, None] * inv_freq[None, :]
    ang = jnp.concatenate([ang_r, ang_r, ang_c, ang_c], axis=-1)
    ang = jnp.tile(ang, (1, LANE // head_dim))
    lane = jnp.arange(LANE)[None, :]
    first = (lane % (2 * q)) < q
    cos, sin = jnp.cos(ang), jnp.sin(ang)
    tab = jnp.concatenate([cos, jnp.where(first, -sin, 0.0), jnp.where(first, 0.0, sin)], axis=-1)
    ident = jnp.concatenate([jnp.ones((s_pad - l_lat, LANE), F32), jnp.zeros((s_pad - l_lat, 2 * LANE), F32)], axis=-1)
    return jnp.concatenate([tab, ident], axis=0)


def kernel(x, c, ctx, c_ctx, w_mod, b_mod, norm1_g, norm2_g, w_in, w_out, da_lam_q1, da_lam_k1, da_lam_q2, da_lam_k2, da_subln_g, sw_sink, ssm_lam_re, ssm_lam_im, ssm_log_dt, ssm_b_re, ssm_b_im, ssm_c_re, ssm_c_im, ssm_d, ssm_w_glu, w_router, w_gate, w_up, w_down, final_g):
    b, l_lat, d = x.shape
    l_ctx = ctx.shape[1]
    depth = w_mod.shape[0]
    ctx_pad = -(-l_ctx // TM) * TM
    s_pad = l_lat + ctx_pad
    n_lat_tiles = l_lat // TM
    assert b == 8 and l_lat % TM == 0 and l_ctx % TQ == 0 and l_lat % GRID_W == 0

    xall = jnp.concatenate([x, ctx, jnp.zeros((b, ctx_pad - l_ctx, d), F32)], axis=1)
    craw = jnp.concatenate([c, c_ctx[None, :], jnp.zeros((16 - b - 1, d), F32)], axis=0)
    mods = _mods(craw, w_mod, b_mod)
    rope = jnp.concatenate([_rope_table(l_lat, s_pad, DA_QK), _rope_table(l_lat, s_pad, SW_HD)], axis=-1)
    tok_lat = _token_columns(l_lat)
    tok_ctx = _token_columns(l_ctx)

    for l in range(depth):
        last = l == depth - 1
        lambda_init = 0.8 - 0.6 * math.exp(-0.3 * l)
        mods4 = mods[l].reshape(16, N_MOD, 1, d)
        p, u = _k1(xall, mods4, norm1_g[l][None, :], w_in[l].astype(BF16), rope, n_lat_tiles)

        lam_params = jnp.zeros((8, LANE), F32).at[0:4, 0:DA_QK].set(
            jnp.stack([da_lam_q1[l], da_lam_k1[l], da_lam_q2[l], da_lam_k2[l]]).astype(F32))
        da = _diff_attention(p, lam_params, da_subln_g[l][None, :].astype(F32), l_lat, l_ctx, lambda_init)
        sw = _window_attention(p, sw_sink[l].astype(F32), l_lat, l_ctx)

        b_blk, c_blk, lam_b = _s5_matrices(ssm_lam_re[l], ssm_lam_im[l], ssm_log_dt[l], ssm_b_re[l], ssm_b_im[l],
                                           ssm_c_re[l], ssm_c_im[l])
        u_rows = u.reshape(s_pad * b, SSM_WIDTH)
        y = _s5_scan(u_rows, b_blk, c_blk, lam_b, b, l_lat, l_ctx)
        ss = _s5_glu(u_rows, y, ssm_d[l][None, :].astype(F32), ssm_w_glu[l].astype(BF16), (l_lat + l_ctx) * b)
        ss = ss.reshape(s_pad, b * SSM_WIDTH)

        w_r = jnp.zeros((d, LANE), BF16).at[:, 0:N_EXPERTS].set(w_router[l].astype(BF16))
        x1, h2, lg = _kout(da, sw, ss, xall, mods4, norm2_g[l][None, :], w_out[l].astype(BF16), w_r, n_lat_tiles)

        idx, gate = _route(lg, tok_lat, 0, l_lat)
        if not last:
            idx_c, gate_c = _route(lg, tok_ctx, l_lat // l_ctx, l_ctx)
            idx = jnp.concatenate([idx, idx_c + l_lat], axis=2)
            gate = jnp.concatenate([gate, gate_c], axis=2)
        xall = _expert_ffn(idx, gate, mods4, w_gate[l].astype(BF16), w_up[l].astype(BF16), w_down[l].astype(BF16),
                           h2, x1, EC_CAPACITY * l_lat // N_EXPERTS)

    return _final_norm(xall, final_g[None, :], l_lat)
```

```python
import functools
import math

import jax
import jax.numpy as jnp
from jax import lax
from jax.experimental import pallas as pl
from jax.experimental.pallas import tpu as pltpu

F32 = jnp.float32
BF16 = jnp.bfloat16

GRID_W = 64
ROPE_THETA = 10000.0
NORM_EPS = 1e-6
NEG_INF = -1e30

DA_HEADS = 6
DA_QK = 64
DA_V = 128
DA_WIDTH = 768
SW_HEADS = 6
SW_KV_HEADS = 2
SW_GROUP = SW_HEADS // SW_KV_HEADS
SW_HD = 128
SW_WINDOW = 128
SW_WIDTH = 768
SSM_GROUPS = 32
SSM_GROUP_CH = 16
SSM_STATE = 64
SSM_WIDTH = 512
SSM_LANES = SSM_GROUPS * SSM_STATE
SSM_RE_MAX = -1e-4
ATT_COLS = 3 * DA_WIDTH + SW_WIDTH + 2 * SW_KV_HEADS * SW_HD
N_EXPERTS = 16
EC_CAPACITY = 2
N_MOD = 6

DA_Q_SCALE = DA_QK ** -0.5 * math.log2(math.e)

TM = 512
TQ = 256
TT = 32
LANE = 128
VMEM_LIMIT = 56 * 1024 * 1024


def _cparams(sem):
    return pltpu.CompilerParams(dimension_semantics=sem, vmem_limit_bytes=VMEM_LIMIT)


def _mods_kernel(c_ref, w_ref, b_ref, o_ref):
    c = c_ref[...]
    s = (c * jax.nn.sigmoid(c)).astype(BF16)
    o_ref[...] = jnp.dot(s, w_ref[...].astype(BF16), preferred_element_type=F32) + b_ref[...]


def _mods(craw, w_mod, b_mod):
    depth, d, n = w_mod.shape
    tn = 1024
    return pl.pallas_call(
        _mods_kernel,
        grid=(depth, n // tn),
        in_specs=[
            pl.BlockSpec((16, d), lambda l, j: (0, 0)),
            pl.BlockSpec((None, d, tn), lambda l, j: (l, 0, j)),
            pl.BlockSpec((None, 1, tn), lambda l, j: (l, 0, j)),
        ],
        out_specs=pl.BlockSpec((None, 16, tn), lambda l, j: (l, 0, j)),
        out_shape=jax.ShapeDtypeStruct((depth, 16, n), F32),
        compiler_params=_cparams(("parallel", "parallel")),
        name="mods",
    )(craw, w_mod, b_mod.reshape(depth, 1, n))


def _rope(acc, cos, sa, sb, quarter):
    return (acc * cos + pltpu.roll(acc, LANE - quarter, 1) * sa + pltpu.roll(acc, quarter, 1) * sb)


def _k1_kernel(x_ref, sh_ref, sc_ref, g_ref, w_ref, rope_ref, p_ref, u_ref):
    x = x_ref[...]
    y = x * lax.rsqrt(jnp.mean(x * x, axis=-1, keepdims=True) + NORM_EPS) * g_ref[...]
    h = (y * (1.0 + sc_ref[...]) + sh_ref[...]).astype(BF16)
    n_chunks = (ATT_COLS + SSM_WIDTH) // 256
    for j in range(n_chunks):
        c0 = j * 256
        acc = jnp.dot(h, w_ref[:, c0:c0 + 256], preferred_element_type=F32)
        if c0 >= ATT_COLS:
            u_ref[:, c0 - ATT_COLS:c0 - ATT_COLS + 256] = acc
            continue
        if c0 < DA_WIDTH:
            t0, quarter = 0, DA_QK // 4
        elif c0 < 2 * DA_WIDTH:
            t0, quarter = 3 * LANE, DA_QK // 4
        elif 3 * DA_WIDTH <= c0 < 3 * DA_WIDTH + SW_WIDTH + SW_KV_HEADS * SW_HD:
            t0, quarter = 6 * LANE, SW_HD // 4
        else:
            t0 = None
        if t0 is not None:
            cos = rope_ref[:, t0:t0 + LANE]
            sa = rope_ref[:, t0 + LANE:t0 + 2 * LANE]
            sb = rope_ref[:, t0 + 2 * LANE:t0 + 3 * LANE]
            acc = jnp.concatenate(
                [_rope(acc[:, :LANE], cos, sa, sb, quarter), _rope(acc[:, LANE:], cos, sa, sb, quarter)], axis=1)
        p_ref[:, c0:c0 + 256] = acc.astype(BF16)


def _k1(xall, mods4, g1, w_in, rope, n_lat_tiles):
    b, s_pad, d = xall.shape
    ncols = w_in.shape[1]

    def mrow(k):
        return lambda bi, i: (jnp.where(i >= n_lat_tiles, 8, bi), k, 0, 0)

    return pl.pallas_call(
        _k1_kernel,
        grid=(b, s_pad // TM),
        in_specs=[
            pl.BlockSpec((None, TM, d), lambda bi, i: (bi, i, 0)),
            pl.BlockSpec((None, None, 1, d), mrow(0)),
            pl.BlockSpec((None, None, 1, d), mrow(1)),
            pl.BlockSpec((1, d), lambda bi, i: (0, 0)),
            pl.BlockSpec((d, ncols), lambda bi, i: (0, 0), pipeline_mode=pl.Buffered(1)),
            pl.BlockSpec((TM, 9 * LANE), lambda bi, i: (i, 0)),
        ],
        out_specs=[
            pl.BlockSpec((None, TM, ATT_COLS), lambda bi, i: (bi, i, 0)),
            pl.BlockSpec((TM, SSM_WIDTH), lambda bi, i: (i, bi)),
        ],
        out_shape=[
            jax.ShapeDtypeStruct((b, s_pad, ATT_COLS), BF16),
            jax.ShapeDtypeStruct((s_pad, b * SSM_WIDTH), F32),
        ],
        compiler_params=_cparams(("parallel", "parallel")),
        name="norm_inproj",
    )(xall, mods4, mods4, g1, w_in, rope)


def _exp2_parts(s):
    m = jnp.max(s, axis=-1, keepdims=True)
    e = jnp.exp2(s - m)
    return e, jnp.sum(e, axis=-1, keepdims=True)


def _da_kernel(q_ref, k_ref, v_ref, lp_ref, g_ref, o_ref, *, n_lat_q, l_lat, l_ctx, lambda_init):
    i = pl.program_id(2)
    lp = lp_ref[...]
    lam = (jnp.exp(jnp.sum(lp[0:1] * lp[1:2], axis=-1, keepdims=True))
           - jnp.exp(jnp.sum(lp[2:3] * lp[3:4], axis=-1, keepdims=True)) + lambda_init)

    def attend(k, v):
        q = q_ref[...]
        lane = lax.broadcasted_iota(jnp.int32, q.shape, 1)
        zero = jnp.zeros_like(q)
        dn = (((1,), (1,)), ((), ()))
        e1, l1 = _exp2_parts(lax.dot_general(jnp.where(lane < DA_QK, q, zero), k, dn, preferred_element_type=F32))
        e2, l2 = _exp2_parts(lax.dot_general(jnp.where(lane >= DA_QK, q, zero), k, dn, preferred_element_type=F32))
        o1 = jnp.dot(e1.astype(BF16), v, preferred_element_type=F32)
        o2 = jnp.dot(e2.astype(BF16), v, preferred_element_type=F32)
        o = o1 * (1.0 / l1) - o2 * (lam / l2)
        o = o * lax.rsqrt(jnp.mean(o * o, axis=-1, keepdims=True) + NORM_EPS) * g_ref[...]
        o_ref[...] = (o * (1.0 - lambda_init)).astype(BF16)

    @pl.when(i < n_lat_q)
    def _():
        attend(k_ref[0:l_lat + l_ctx, :], v_ref[0:l_lat + l_ctx, :])

    @pl.when(i == n_lat_q)
    def _():
        attend(k_ref[l_lat:l_lat + l_ctx, :], v_ref[l_lat:l_lat + l_ctx, :])

    @pl.when(i > n_lat_q)
    def _():
        o_ref[...] = jnp.zeros_like(o_ref)


def _diff_attention(p, lam_params, subln_g, l_lat, l_ctx, lambda_init):
    b, s_pad, _ = p.shape
    kern = functools.partial(_da_kernel, n_lat_q=l_lat // TQ, l_lat=l_lat, l_ctx=l_ctx, lambda_init=lambda_init)
    return pl.pallas_call(
        kern,
        grid=(b, DA_HEADS, s_pad // TQ),
        in_specs=[
            pl.BlockSpec((None, TQ, LANE), lambda bi, h, i: (bi, i, h)),
            pl.BlockSpec((None, s_pad, LANE), lambda bi, h, i: (bi, 0, DA_HEADS + h)),
            pl.BlockSpec((None, s_pad, LANE), lambda bi, h, i: (bi, 0, 2 * DA_HEADS + h)),
            pl.BlockSpec((8, LANE), lambda bi, h, i: (0, 0)),
            pl.BlockSpec((1, LANE), lambda bi, h, i: (0, 0)),
        ],
        out_specs=pl.BlockSpec((None, TQ, LANE), lambda bi, h, i: (bi, i, h)),
        out_shape=jax.ShapeDtypeStruct((b, s_pad, DA_WIDTH), BF16),
        compiler_params=_cparams(("parallel", "parallel", "arbitrary")),
        name="diff_attention",
    )(p, p, p, lam_params, subln_g)


def _sw_kernel(sink_ref, q_ref, k_ref, v_ref, o_ref, *, n_lat_q, l_lat, l_ctx):
    i = pl.program_id(2)
    kvh = pl.program_id(1)
    scale = SW_HD ** -0.5
    dn = (((1,), (1,)), ((), ()))
    band_w = TQ + 2 * SW_WINDOW
    kc = k_ref[l_lat:l_lat + l_ctx, :]
    vc = v_ref[l_lat:l_lat + l_ctx, :]

    def ctx_scores(qg):
        return lax.dot_general(qg, kc, dn, preferred_element_type=F32) * scale

    @pl.when(i < n_lat_q)
    def _():
        q0 = i * TQ
        kstart = pl.multiple_of(jnp.clip(q0 - SW_WINDOW, 0, l_lat - band_w), SW_WINDOW)
        kb = k_ref[pl.ds(kstart, band_w), :]
        vb = v_ref[pl.ds(kstart, band_w), :]
        qpos = q0 + lax.broadcasted_iota(jnp.int32, (TQ, band_w), 0)
        kpos = kstart + lax.broadcasted_iota(jnp.int32, (TQ, band_w), 1)
        in_band = jnp.abs(qpos - kpos) <= SW_WINDOW
        for g in range(SW_GROUP):
            qg = q_ref[:, g * SW_HD:(g + 1) * SW_HD]
            sink = sink_ref[kvh * SW_GROUP + g]
            sb = jnp.where(in_band, lax.dot_general(qg, kb, dn, preferred_element_type=F32) * scale, NEG_INF)
            sc = ctx_scores(qg)
            m = jnp.maximum(jnp.maximum(jnp.max(sb, axis=-1, keepdims=True), jnp.max(sc, axis=-1, keepdims=True)), sink)
            eb = jnp.exp(sb - m)
            ec = jnp.exp(sc - m)
            denom = jnp.sum(eb, axis=-1, keepdims=True) + jnp.sum(ec, axis=-1, keepdims=True) + jnp.exp(sink - m)
            inv = 1.0 / denom
            o = (jnp.dot((eb * inv).astype(BF16), vb, preferred_element_type=F32)
                 + jnp.dot((ec * inv).astype(BF16), vc, preferred_element_type=F32))
            o_ref[:, g * SW_HD:(g + 1) * SW_HD] = o.astype(BF16)

    @pl.when(i == n_lat_q)
    def _():
        for g in range(SW_GROUP):
            qg = q_ref[:, g * SW_HD:(g + 1) * SW_HD]
            sink = sink_ref[kvh * SW_GROUP + g]
            sc = ctx_scores(qg)
            m = jnp.maximum(jnp.max(sc, axis=-1, keepdims=True), sink)
            ec = jnp.exp(sc - m)
            inv = 1.0 / (jnp.sum(ec, axis=-1, keepdims=True) + jnp.exp(sink - m))
            o = jnp.dot((ec * inv).astype(BF16), vc, preferred_element_type=F32)
            o_ref[:, g * SW_HD:(g + 1) * SW_HD] = o.astype(BF16)

    @pl.when(i > n_lat_q)
    def _():
        o_ref[...] = jnp.zeros_like(o_ref)


def _window_attention(p, sink, l_lat, l_ctx):
    b, s_pad, _ = p.shape
    gw = SW_GROUP * SW_HD
    q_blk0 = 3 * DA_WIDTH // gw
    k_blk0 = (3 * DA_WIDTH + SW_WIDTH) // SW_HD
    v_blk0 = k_blk0 + SW_KV_HEADS
    kern = functools.partial(_sw_kernel, n_lat_q=l_lat // TQ, l_lat=l_lat, l_ctx=l_ctx)
    return pl.pallas_call(
        kern,
        grid=(b, SW_KV_HEADS, s_pad // TQ),
        in_specs=[
            pl.BlockSpec(memory_space=pltpu.SMEM),
            pl.BlockSpec((None, TQ, gw), lambda bi, h, i: (bi, i, q_blk0 + h)),
            pl.BlockSpec((None, s_pad, SW_HD), lambda bi, h, i: (bi, 0, k_blk0 + h)),
            pl.BlockSpec((None, s_pad, SW_HD), lambda bi, h, i: (bi, 0, v_blk0 + h)),
        ],
        out_specs=pl.BlockSpec((None, TQ, gw), lambda bi, h, i: (bi, i, h)),
        out_shape=jax.ShapeDtypeStruct((b, s_pad, SW_WIDTH), BF16),
        compiler_params=_cparams(("parallel", "parallel", "arbitrary")),
        name="window_attention",
    )(sink, p, p, p)


def _s5_discretise(lam_re, lam_im, log_dt, b_re, b_im):
    lr = jnp.minimum(lam_re.astype(F32), SSM_RE_MAX)
    li = lam_im.astype(F32)
    dt = jnp.exp(log_dt.astype(F32))[..., None]
    mag = jnp.exp(lr * dt)
    lbr = mag * jnp.cos(li * dt)
    lbi = mag * jnp.sin(li * dt)
    den = lr * lr + li * li
    cr = ((lbr - 1.0) * lr + lbi * li) / den
    ci = (lbi * lr - (lbr - 1.0) * li) / den
    br = b_re.astype(F32)
    bi = b_im.astype(F32)
    bbr = cr[..., None] * br - ci[..., None] * bi
    bbi = cr[..., None] * bi + ci[..., None] * br
    return lbr, lbi, bbr, bbi


def _s5_matrices(lam_re, lam_im, log_dt, b_re, b_im, c_re, c_im):
    lbr, lbi, bbr, bbi = _s5_discretise(lam_re, lam_im, log_dt, b_re, b_im)
    eye = jnp.eye(SSM_GROUPS, dtype=F32)

    def in_map(bb):
        return jnp.einsum("dgnp,gh->dgphn", bb, eye).reshape(2, SSM_WIDTH, SSM_LANES)

    def out_map(cc):
        return jnp.einsum("dgpn,gh->dgnhp", cc, eye).reshape(2, SSM_LANES, SSM_WIDTH)

    b_blk = jnp.concatenate([in_map(bbr), in_map(bbi)], axis=2).astype(BF16)
    c_blk = jnp.concatenate([out_map(c_re.astype(F32)), out_map(-c_im.astype(F32))], axis=1).astype(BF16)
    lam_b = jnp.concatenate([lbr.reshape(2, 1, SSM_LANES), lbi.reshape(2, 1, SSM_LANES)], axis=2)
    lam_b = jnp.broadcast_to(lam_b, (2, 8, 2 * SSM_LANES))
    return b_blk, c_blk, lam_b


def _s5_kernel(u_ref, b_ref, c_ref, lam_ref, y_ref, bu_ref, hs_ref, h_ref, *, n_batch):
    d = pl.program_id(0)
    s = pl.program_id(1)
    rows = TT * n_batch

    @pl.when(s == 0)
    def _():
        h_ref[...] = jnp.zeros_like(h_ref)

    u = u_ref[...].astype(BF16)
    for c in range(2 * SSM_LANES // 256):
        k0 = ((c % (SSM_LANES // 256)) * 256 // SSM_STATE * SSM_GROUP_CH) // LANE * LANE
        bu_ref[:, c * 256:(c + 1) * 256] = jnp.dot(
            u[:, k0:k0 + LANE], b_ref[k0:k0 + LANE, c * 256:(c + 1) * 256], preferred_element_type=F32)

    cw = 512
    for c in range(SSM_LANES // cw):
        re = slice(c * cw, (c + 1) * cw)
        im = slice(SSM_LANES + c * cw, SSM_LANES + (c + 1) * cw)
        lr = lam_ref[:, re]
        li = lam_ref[:, im]

        def step(t, carry, re=re, im=im, lr=lr, li=li):
            hr, hi = carry
            tt = jnp.where(d == 0, t, TT - 1 - t)
            r = pl.multiple_of(tt * n_batch, n_batch)
            nr = lr * hr - li * hi + bu_ref[pl.ds(r, n_batch), re]
            ni = lr * hi + li * hr + bu_ref[pl.ds(r, n_batch), im]
            hs_ref[pl.ds(r, n_batch), re] = nr
            hs_ref[pl.ds(r, n_batch), im] = ni
            return nr, ni

        hr, hi = lax.fori_loop(0, TT, step, (h_ref[:, re], h_ref[:, im]), unroll=4)
        h_ref[:, re] = hr
        h_ref[:, im] = hi

    half = SSM_LANES // 2
    for j in range(SSM_WIDTH // 256):
        acc = jnp.dot(hs_ref[:, j * half:(j + 1) * half].astype(BF16),
                      c_ref[j * half:(j + 1) * half, j * 256:(j + 1) * 256], preferred_element_type=F32)
        acc += jnp.dot(hs_ref[:, SSM_LANES + j * half:SSM_LANES + (j + 1) * half].astype(BF16),
                       c_ref[SSM_LANES + j * half:SSM_LANES + (j + 1) * half, j * 256:(j + 1) * 256],
                       preferred_element_type=F32)
        y_ref[:, j * 256:(j + 1) * 256] = acc


def _s5_scan(u_rows, b_blk, c_blk, lam_b, n_batch, l_lat, l_ctx):
    rows = TT * n_batch
    n_lat, n_ctx = l_lat // TT, l_ctx // TT
    n_steps = n_lat + n_ctx

    def tile(d, s):
        fwd = jnp.where(s < n_ctx, n_lat + s, s - n_ctx)
        bwd = jnp.where(s < n_ctx, n_lat + n_ctx - 1 - s, n_lat - 1 - (s - n_ctx))
        return jnp.where(d == 0, fwd, bwd)

    kern = functools.partial(_s5_kernel, n_batch=n_batch)
    return pl.pallas_call(
        kern,
        grid=(2, n_steps),
        in_specs=[
            pl.BlockSpec((rows, SSM_WIDTH), lambda d, s: (tile(d, s), 0)),
            pl.BlockSpec((None, SSM_WIDTH, 2 * SSM_LANES), lambda d, s: (d, 0, 0)),
            pl.BlockSpec((None, 2 * SSM_LANES, SSM_WIDTH), lambda d, s: (d, 0, 0)),
            pl.BlockSpec((None, 8, 2 * SSM_LANES), lambda d, s: (d, 0, 0)),
        ],
        out_specs=pl.BlockSpec((None, rows, SSM_WIDTH), lambda d, s: (d, tile(d, s), 0)),
        out_shape=jax.ShapeDtypeStruct((2, (l_lat + l_ctx) * n_batch, SSM_WIDTH), F32),
        scratch_shapes=[
            pltpu.VMEM((rows, 2 * SSM_LANES), F32),
            pltpu.VMEM((rows, 2 * SSM_LANES), F32),
            pltpu.VMEM((n_batch, 2 * SSM_LANES), F32),
        ],
        compiler_params=_cparams(("arbitrary", "arbitrary")),
        name="s5_scan",
    )(u_rows, b_blk, c_blk, lam_b)


def _glu_kernel(u_ref, y_ref, d_ref, w_ref, o_ref, *, n_valid_tiles):
    @pl.when(pl.program_id(0) < n_valid_tiles)
    def _():
        y = u_ref[...] * d_ref[...] + y_ref[0] + y_ref[1]
        g = jax.nn.gelu(y)
        z = jnp.dot(g.astype(BF16), w_ref[...], preferred_element_type=F32)
        o_ref[...] = (g * jax.nn.sigmoid(z)).astype(BF16)

    @pl.when(pl.program_id(0) >= n_valid_tiles)
    def _():
        o_ref[...] = jnp.zeros_like(o_ref)


def _s5_glu(u_rows, y, d_skip, w_glu, n_valid_rows):
    n_rows = u_rows.shape[0]
    tr = 512
    n_valid_tiles = n_valid_rows // tr
    return pl.pallas_call(
        functools.partial(_glu_kernel, n_valid_tiles=n_valid_tiles),
        grid=(n_rows // tr,),
        in_specs=[
            pl.BlockSpec((tr, SSM_WIDTH), lambda i: (i, 0)),
            pl.BlockSpec((2, tr, SSM_WIDTH), lambda i: (0, jnp.minimum(i, n_valid_tiles - 1), 0)),
            pl.BlockSpec((1, SSM_WIDTH), lambda i: (0, 0)),
            pl.BlockSpec((SSM_WIDTH, SSM_WIDTH), lambda i: (0, 0)),
        ],
        out_specs=pl.BlockSpec((tr, SSM_WIDTH), lambda i: (i, 0)),
        out_shape=jax.ShapeDtypeStruct((n_rows, SSM_WIDTH), BF16),
        compiler_params=_cparams(("parallel",)),
        name="s5_glu",
    )(u_rows, y, d_skip, w_glu)


def _kout_kernel(da_ref, sw_ref, ss_ref, x_ref, gate_ref, sh_ref, sc_ref, g_ref, w_ref, wr_ref,
                 x1_ref, h2_ref, lg_ref):
    acc = jnp.dot(da_ref[...], w_ref[0:DA_WIDTH, :], preferred_element_type=F32)
    acc += jnp.dot(sw_ref[...], w_ref[DA_WIDTH:DA_WIDTH + SW_WIDTH, :], preferred_element_type=F32)
    acc += jnp.dot(ss_ref[...], w_ref[DA_WIDTH + SW_WIDTH:, :], preferred_element_type=F32)
    x1 = x_ref[...] + gate_ref[...] * acc
    x1_ref[...] = x1
    y = x1 * lax.rsqrt(jnp.mean(x1 * x1, axis=-1, keepdims=True) + NORM_EPS) * g_ref[...]
    h2 = y * (1.0 + sc_ref[...]) + sh_ref[...]
    h2_ref[...] = h2
    lg_ref[...] = jnp.dot(h2.astype(BF16), wr_ref[...], preferred_element_type=F32)


def _kout(da, sw, ss, xall, mods4, g2, w_out, w_router, n_lat_tiles):
    b, s_pad, d = xall.shape

    def mrow(k):
        return lambda bi, i: (jnp.where(i >= n_lat_tiles, 8, bi), k, 0, 0)

    tok = lambda bi, i: (bi, i, 0)
    return pl.pallas_call(
        _kout_kernel,
        grid=(b, s_pad // TM),
        in_specs=[
            pl.BlockSpec((None, TM, DA_WIDTH), tok),
            pl.BlockSpec((None, TM, SW_WIDTH), tok),
            pl.BlockSpec((TM, SSM_WIDTH), lambda bi, i: (i, bi)),
            pl.BlockSpec((None, TM, d), tok),
            pl.BlockSpec((None, None, 1, d), mrow(2)),
            pl.BlockSpec((None, None, 1, d), mrow(3)),
            pl.BlockSpec((None, None, 1, d), mrow(4)),
            pl.BlockSpec((1, d), lambda bi, i: (0, 0)),
            pl.BlockSpec((d, d), lambda bi, i: (0, 0), pipeline_mode=pl.Buffered(1)),
            pl.BlockSpec((d, LANE), lambda bi, i: (0, 0)),
        ],
        out_specs=[
            pl.BlockSpec((None, TM, d), tok),
            pl.BlockSpec((None, TM, d), tok),
            pl.BlockSpec((None, TM, LANE), tok),
        ],
        out_shape=[
            jax.ShapeDtypeStruct((b, s_pad, d), F32),
            jax.ShapeDtypeStruct((b, s_pad, d), F32),
            jax.ShapeDtypeStruct((b, s_pad, LANE), F32),
        ],
        compiler_params=_cparams(("parallel", "parallel")),
        name="outproj_norm_router",
    )(da, sw, ss, xall, mods4, mods4, mods4, g2, w_out, w_router)


def _lane_cumsum(x, tri):
    n = x.shape[1]
    total = jnp.zeros((x.shape[0], 1), F32)
    parts = []
    for c in range(n // 256):
        loc = jnp.dot(x[:, c * 256:(c + 1) * 256].astype(BF16), tri, preferred_element_type=F32)
        parts.append(loc + total)
        total = total + loc[:, 255:256]
    return jnp.concatenate(parts, axis=1)


def _route_kernel(lg_ref, tok_ref, idx_ref, gate_ref, *, n_tok, cap):
    lg = lg_ref[...]
    lane = lax.broadcasted_iota(jnp.int32, lg.shape, 1)
    lg = jnp.where(lane < N_EXPERTS, lg, NEG_INF)
    e = jnp.exp(lg - jnp.max(lg, axis=-1, keepdims=True))
    aff = e / jnp.sum(e, axis=-1, keepdims=True)
    aff_t = jnp.transpose(aff)[0:N_EXPERTS, :]
    bits = pltpu.bitcast(aff_t, jnp.int32)

    def search(k, thr):
        cand = thr | lax.shift_left(jnp.int32(1), 30 - k)
        cnt = jnp.sum((bits >= cand).astype(F32), axis=-1, keepdims=True)
        return jnp.where(cnt >= cap, cand, thr)

    thr = lax.fori_loop(0, 31, search, jnp.zeros((N_EXPERTS, 1), jnp.int32))
    gt = bits > thr
    eq = bits == thr
    r_i = lax.broadcasted_iota(jnp.int32, (256, 256), 0)
    c_i = lax.broadcasted_iota(jnp.int32, (256, 256), 1)
    tri = (r_i <= c_i).astype(BF16)
    need = cap - jnp.sum(gt.astype(F32), axis=-1, keepdims=True)
    eq_f = eq.astype(F32)
    eq_rank = _lane_cumsum(eq_f, tri) - eq_f
    sel = jnp.where(gt, 1.0, jnp.where(eq & (eq_rank < need), 1.0, 0.0))
    pos = _lane_cumsum(sel, tri) - sel
    pos = jnp.where(sel > 0.0, pos, -1.0)

    a_hi = aff.astype(BF16)
    r1 = aff - a_hi.astype(F32)
    a_mid = r1.astype(BF16)
    a_lo = (r1 - a_mid.astype(F32)).astype(BF16)
    tok = tok_ref[...]
    payload = jnp.where(lane < 16, a_hi.astype(F32),
                        jnp.where(lane < 32, pltpu.roll(a_mid.astype(F32), 16, 1),
                                  jnp.where(lane < 48, pltpu.roll(a_lo.astype(F32), 32, 1), tok))).astype(BF16)

    slot = lax.broadcasted_iota(jnp.int32, (cap, 256), 0).astype(F32)
    out_lane = lax.broadcasted_iota(jnp.int32, (cap, LANE), 1)
    for ex in range(N_EXPERTS):
        acc = jnp.zeros((cap, LANE), F32)
        for c in range(n_tok // 256):
            onehot = jnp.where(pos[ex:ex + 1, c * 256:(c + 1) * 256] == slot, 1.0, 0.0).astype(BF16)
            acc += jnp.dot(onehot, payload[c * 256:(c + 1) * 256, :], preferred_element_type=F32)
        pick = lambda l: jnp.sum(jnp.where(out_lane == l, acc, 0.0), axis=-1, keepdims=True)
        gate_ref[ex] = (pick(ex) + pick(16 + ex)) + pick(32 + ex)
        idx_ref[ex] = (pick(48) * 64.0 + pick(49)).astype(jnp.int32)


def _route(lg, tok_cols, row_block, n_tok):
    b = lg.shape[0]
    cap = EC_CAPACITY * n_tok // N_EXPERTS
    kern = functools.partial(_route_kernel, n_tok=n_tok, cap=cap)
    idx, gate = pl.pallas_call(
        kern,
        grid=(b,),
        in_specs=[
            pl.BlockSpec((None, n_tok, LANE), lambda bi: (bi, row_block, 0)),
            pl.BlockSpec((n_tok, LANE), lambda bi: (0, 0)),
        ],
        out_specs=[
            pl.BlockSpec((None, N_EXPERTS, cap, 1), lambda bi: (bi, 0, 0, 0)),
            pl.BlockSpec((None, N_EXPERTS, cap, 1), lambda bi: (bi, 0, 0, 0)),
        ],
        out_shape=[
            jax.ShapeDtypeStruct((b, N_EXPERTS, cap, 1), jnp.int32),
            jax.ShapeDtypeStruct((b, N_EXPERTS, cap, 1), F32),
        ],
        compiler_params=_cparams(("parallel",)),
        name="route",
    )(lg, tok_cols)
    return idx[..., 0], gate


def _token_columns(n_tok):
    t = jnp.arange(n_tok, dtype=jnp.int32)[:, None]
    lane = jnp.arange(LANE, dtype=jnp.int32)[None, :]
    return jnp.where(lane == 48, t // 64, jnp.where(lane == 49, t % 64, 0)).astype(F32)


X_SLOTS = 2
ACC_SLOTS = 3
DMA_UNROLL = 8


def _ffn_kernel(idx_ref, idx_next_ref, gate_ref, m5_ref, m5c_ref, wg_ref, wu_ref, wd_ref, h2_hbm, acc_in_hbm,
                acc_hbm, xbuf, abuf, sem_x, sem_a, sem_o, *, n_rows, n_lat_rows, n_batch):
    del acc_in_hbm
    b = pl.program_id(1)
    s = pl.program_id(0) * n_batch + b
    n_steps = N_EXPERTS * n_batch
    xs = s % X_SLOTS
    sa = s % ACC_SLOTS

    def start_gathers(idx_r, sample, x_slot, a_slot):
        def body(c, carry):
            t = idx_r[0, c]
            pltpu.make_async_copy(h2_hbm.at[sample, pl.ds(t, 1), :], xbuf.at[x_slot, pl.ds(c, 1), :],
                                  sem_x.at[x_slot]).start()
            pltpu.make_async_copy(acc_hbm.at[sample, pl.ds(t, 1), :], abuf.at[a_slot, pl.ds(c, 1), :],
                                  sem_a.at[a_slot]).start()
            return carry

        lax.fori_loop(0, n_rows, body, 0, unroll=DMA_UNROLL)

    def wait_rows_in(buf, slot, sem):
        pltpu.make_async_copy(h2_hbm.at[0, pl.ds(0, n_rows), :], buf.at[slot], sem.at[slot]).wait()

    def wait_rows_out(slot):
        pltpu.make_async_copy(abuf.at[slot], acc_hbm.at[0, pl.ds(0, n_rows), :], sem_o.at[slot]).wait()

    @pl.when(s == 0)
    def _():
        start_gathers(idx_ref, b, 0, 0)

    @pl.when(s >= 2)
    def _():
        wait_rows_out((s + 1) % ACC_SLOTS)

    @pl.when(s + 1 < n_steps)
    def _():
        start_gathers(idx_next_ref, (b + 1) % n_batch, (s + 1) % X_SLOTS, (s + 1) % ACC_SLOTS)

    wait_rows_in(xbuf, xs, sem_x)
    wait_rows_in(abuf, sa, sem_a)

    x = xbuf[xs].astype(BF16)
    hg = jnp.dot(x, wg_ref[...], preferred_element_type=F32)
    hu = jnp.dot(x, wu_ref[...], preferred_element_type=F32)
    hid = (hg * jax.nn.sigmoid(hg) * hu).astype(BF16)
    y = jnp.dot(hid, wd_ref[...], preferred_element_type=F32)
    row = lax.broadcasted_iota(jnp.int32, (n_rows, 1), 0)
    m5 = jnp.where(row < n_lat_rows, m5_ref[...], m5c_ref[...])
    abuf[sa] = abuf[sa] + m5 * (y * gate_ref[...])

    def scatter(c, carry):
        pltpu.make_async_copy(abuf.at[sa, pl.ds(c, 1), :], acc_hbm.at[b, pl.ds(idx_ref[0, c], 1), :],
                              sem_o.at[sa]).start()
        return carry

    lax.fori_loop(0, n_rows, scatter, 0, unroll=DMA_UNROLL)

    @pl.when(s == n_steps - 1)
    def _():
        wait_rows_out((s - 1) % ACC_SLOTS)
        wait_rows_out(sa)


def _expert_ffn(idx, gate, mods4, w_gate, w_up, w_down, h2, x1, n_lat_rows):
    b, s_pad, d = x1.shape
    n_rows = idx.shape[2]
    ff = w_gate.shape[2]
    kern = functools.partial(_ffn_kernel, n_rows=n_rows, n_lat_rows=n_lat_rows, n_batch=b)
    idx4 = idx[:, :, None, :]

    def next_step(e, bi):
        return (bi + 1) % b, jnp.minimum(e + (bi + 1) // b, N_EXPERTS - 1), 0, 0

    weight = pl.BlockSpec
    return pl.pallas_call(
        kern,
        grid=(N_EXPERTS, b),
        in_specs=[
            pl.BlockSpec((None, None, 1, n_rows), lambda e, bi: (bi, e, 0, 0), memory_space=pltpu.SMEM),
            pl.BlockSpec((None, None, 1, n_rows), next_step, memory_space=pltpu.SMEM),
            pl.BlockSpec((None, None, n_rows, 1), lambda e, bi: (bi, e, 0, 0)),
            pl.BlockSpec((None, None, 1, d), lambda e, bi: (bi, 5, 0, 0)),
            pl.BlockSpec((None, None, 1, d), lambda e, bi: (8, 5, 0, 0)),
            weight((None, d, ff), lambda e, bi: (e, 0, 0)),
            weight((None, d, ff), lambda e, bi: (e, 0, 0)),
            weight((None, ff, d), lambda e, bi: (e, 0, 0)),
            pl.BlockSpec(memory_space=pl.ANY),
            pl.BlockSpec(memory_space=pl.ANY),
        ],
        out_specs=pl.BlockSpec(memory_space=pl.ANY),
        out_shape=jax.ShapeDtypeStruct((b, s_pad, d), F32),
        scratch_shapes=[
            pltpu.VMEM((X_SLOTS, n_rows, d), F32),
            pltpu.VMEM((ACC_SLOTS, n_rows, d), F32),
            pltpu.SemaphoreType.DMA((X_SLOTS,)),
            pltpu.SemaphoreType.DMA((ACC_SLOTS,)),
            pltpu.SemaphoreType.DMA((ACC_SLOTS,)),
        ],
        input_output_aliases={9: 0},
        compiler_params=pltpu.CompilerParams(
            dimension_semantics=("arbitrary", "arbitrary"), vmem_limit_bytes=VMEM_LIMIT, has_side_effects=True),
        name="expert_ffn",
    )(idx4, idx4, gate, mods4, mods4, w_gate, w_up, w_down, h2, x1)


def _final_kernel(x_ref, g_ref, o_ref):
    x = x_ref[...]
    o_ref[...] = x * lax.rsqrt(jnp.mean(x * x, axis=-1, keepdims=True) + NORM_EPS) * g_ref[...]


def _final_norm(xall, g, l_lat):
    b, _, d = xall.shape
    return pl.pallas_call(
        _final_kernel,
        grid=(b, l_lat // TM),
        in_specs=[pl.BlockSpec((None, TM, d), lambda bi, i: (bi, i, 0)), pl.BlockSpec((1, d), lambda bi, i: (0, 0))],
        out_specs=pl.BlockSpec((None, TM, d), lambda bi, i: (bi, i, 0)),
        out_shape=jax.ShapeDtypeStruct((b, l_lat, d), F32),
        compiler_params=_cparams(("parallel", "parallel")),
        name="final_norm",
    )(xall, g)


def _rope_table(l_lat, s_pad, head_dim):
    q = head_dim // 4
    inv_freq = ROPE_THETA ** (-jnp.arange(q, dtype=F32) / q)
    pos = jnp.arange(l_lat)
    ang_r = (pos // GRID_W).astype(F32)[:, None] * inv_freq[None, :]
    ang_c = (pos % GRID_W).astype(F32)[:, None] * inv_freq[None, :]
    ang = jnp.concatenate([ang_r, ang_r, ang_c, ang_c], axis=-1)
    ang = jnp.tile(ang, (1, LANE // head_dim))
    lane = jnp.arange(LANE)[None, :]
    first = (lane % (2 * q)) < q
    cos, sin = jnp.cos(ang), jnp.sin(ang)
    tab = jnp.concatenate([cos, jnp.where(first, -sin, 0.0), jnp.where(first, 0.0, sin)], axis=-1)
    ident = jnp.concatenate([jnp.ones((s_pad - l_lat, LANE), F32), jnp.zeros((s_pad - l_lat, 2 * LANE), F32)], axis=-1)
    return jnp.concatenate([tab, ident], axis=0)


def kernel(x, c, ctx, c_ctx, w_mod, b_mod, norm1_g, norm2_g, w_in, w_out, da_lam_q1, da_lam_k1, da_lam_q2, da_lam_k2, da_subln_g, sw_sink, ssm_lam_re, ssm_lam_im, ssm_log_dt, ssm_b_re, ssm_b_im, ssm_c_re, ssm_c_im, ssm_d, ssm_w_glu, w_router, w_gate, w_up, w_down, final_g):
    b, l_lat, d = x.shape
    l_ctx = ctx.shape[1]
    depth = w_mod.shape[0]
    ctx_pad = -(-l_ctx // TM) * TM
    s_pad = l_lat + ctx_pad
    n_lat_tiles = l_lat // TM
    assert b == 8 and l_lat % TM == 0 and l_ctx % TQ == 0 and l_lat % GRID_W == 0

    xall = jnp.concatenate([x, ctx, jnp.zeros((b, ctx_pad - l_ctx, d), F32)], axis=1)
    craw = jnp.concatenate([c, c_ctx[None, :], jnp.zeros((16 - b - 1, d), F32)], axis=0)
    mods = _mods(craw, w_mod, b_mod)
    rope_da = _rope_table(l_lat, s_pad, DA_QK)
    rope = jnp.concatenate([rope_da * DA_Q_SCALE, rope_da, _rope_table(l_lat, s_pad, SW_HD)], axis=-1)
    tok_lat = _token_columns(l_lat)
    tok_ctx = _token_columns(l_ctx)

    for l in range(depth):
        last = l == depth - 1
        lambda_init = 0.8 - 0.6 * math.exp(-0.3 * l)
        mods4 = mods[l].reshape(16, N_MOD, 1, d)
        p, u = _k1(xall, mods4, norm1_g[l][None, :], w_in[l].astype(BF16), rope, n_lat_tiles)

        lam_params = jnp.zeros((8, LANE), F32).at[0:4, 0:DA_QK].set(
            jnp.stack([da_lam_q1[l], da_lam_k1[l], da_lam_q2[l], da_lam_k2[l]]).astype(F32))
        da = _diff_attention(p, lam_params, da_subln_g[l][None, :].astype(F32), l_lat, l_ctx, lambda_init)
        sw = _window_attention(p, sw_sink[l].astype(F32), l_lat, l_ctx)

        b_blk, c_blk, lam_b = _s5_matrices(ssm_lam_re[l], ssm_lam_im[l], ssm_log_dt[l], ssm_b_re[l], ssm_b_im[l],
                                           ssm_c_re[l], ssm_c_im[l])
        u_rows = u.reshape(s_pad * b, SSM_WIDTH)
        y = _s5_scan(u_rows, b_blk, c_blk, lam_b, b, l_lat, l_ctx)
        ss = _s5_glu(u_rows, y, ssm_d[l][None, :].astype(F32), ssm_w_glu[l].astype(BF16), (l_lat + l_ctx) * b)
        ss = ss.reshape(s_pad, b * SSM_WIDTH)

        w_r = jnp.zeros((d, LANE), BF16).at[:, 0:N_EXPERTS].set(w_router[l].astype(BF16))
        x1, h2, lg = _kout(da, sw, ss, xall, mods4, norm2_g[l][None, :], w_out[l].astype(BF16), w_r, n_lat_tiles)

        idx, gate = _route(lg, tok_lat, 0, l_lat)
        if not last:
            idx_c, gate_c = _route(lg, tok_ctx, l_lat // l_ctx, l_ctx)
            idx = jnp.concatenate([idx, idx_c + l_lat], axis=2)
            gate = jnp.concatenate([gate, gate_c], axis=2)
        xall = _expert_ffn(idx, gate, mods4, w_gate[l].astype(BF16), w_up[l].astype(BF16), w_down[l].astype(BF16),
                           h2, x1, EC_CAPACITY * l_lat // N_EXPERTS)

    return _final_norm(xall, final_g[None, :], l_lat)
```

```python
import functools
import math

import jax
import jax.numpy as jnp
from jax import lax
from jax.experimental import pallas as pl
from jax.experimental.pallas import tpu as pltpu

F32 = jnp.float32
BF16 = jnp.bfloat16

GRID_W = 64
ROPE_THETA = 10000.0
NORM_EPS = 1e-6
NEG_INF = -1e30

DA_HEADS = 6
DA_QK = 64
DA_V = 128
DA_WIDTH = 768
SW_HEADS = 6
SW_KV_HEADS = 2
SW_GROUP = SW_HEADS // SW_KV_HEADS
SW_HD = 128
SW_WINDOW = 128
SW_WIDTH = 768
SSM_GROUPS = 32
SSM_GROUP_CH = 16
SSM_STATE = 64
SSM_WIDTH = 512
SSM_LANES = SSM_GROUPS * SSM_STATE
SSM_RE_MAX = -1e-4
ATT_COLS = 3 * DA_WIDTH + SW_WIDTH + 2 * SW_KV_HEADS * SW_HD
N_EXPERTS = 16
EC_CAPACITY = 2
N_MOD = 6

LOG2_E = math.log2(math.e)
DA_Q_SCALE = DA_QK ** -0.5 * LOG2_E
SW_Q_SCALE = SW_HD ** -0.5 * LOG2_E

TM = 512
TQ = 256
TT = 32
LANE = 128
VMEM_LIMIT = 56 * 1024 * 1024


def _cparams(sem):
    return pltpu.CompilerParams(dimension_semantics=sem, vmem_limit_bytes=VMEM_LIMIT)


def _mods_kernel(c_ref, w_ref, b_ref, o_ref):
    c = c_ref[...]
    s = (c * jax.nn.sigmoid(c)).astype(BF16)
    o_ref[...] = jnp.dot(s, w_ref[...].astype(BF16), preferred_element_type=F32) + b_ref[...]


def _mods(craw, w_mod, b_mod):
    depth, d, n = w_mod.shape
    tn = 1024
    return pl.pallas_call(
        _mods_kernel,
        grid=(depth, n // tn),
        in_specs=[
            pl.BlockSpec((16, d), lambda l, j: (0, 0)),
            pl.BlockSpec((None, d, tn), lambda l, j: (l, 0, j)),
            pl.BlockSpec((None, 1, tn), lambda l, j: (l, 0, j)),
        ],
        out_specs=pl.BlockSpec((None, 16, tn), lambda l, j: (l, 0, j)),
        out_shape=jax.ShapeDtypeStruct((depth, 16, n), F32),
        compiler_params=_cparams(("parallel", "parallel")),
        name="mods",
    )(craw, w_mod, b_mod.reshape(depth, 1, n))


def _rope(acc, cos, sa, sb, quarter):
    return (acc * cos + pltpu.roll(acc, LANE - quarter, 1) * sa + pltpu.roll(acc, quarter, 1) * sb)


ROW_SPLIT = 2


def _k1_kernel(x_ref, sh_ref, sc_ref, g_ref, w_ref, rope_ref, p_ref, u_ref):
    n_chunks = (ATT_COLS + SSM_WIDTH) // 256
    hr = TM // ROW_SPLIT
    for r in range(ROW_SPLIT):
        rows = slice(r * hr, (r + 1) * hr)
        x = x_ref[rows, :]
        y = x * lax.rsqrt(jnp.mean(x * x, axis=-1, keepdims=True) + NORM_EPS) * g_ref[...]
        h = (y * (1.0 + sc_ref[...]) + sh_ref[...]).astype(BF16)
        for j in range(n_chunks):
            c0 = j * 256
            acc = jnp.dot(h, w_ref[:, c0:c0 + 256], preferred_element_type=F32)
            if c0 >= ATT_COLS:
                u_ref[rows, c0 - ATT_COLS:c0 - ATT_COLS + 256] = acc
                continue
            if c0 < DA_WIDTH:
                t0, quarter = 0, DA_QK // 4
            elif c0 < 2 * DA_WIDTH:
                t0, quarter = 3 * LANE, DA_QK // 4
            elif 3 * DA_WIDTH <= c0 < 3 * DA_WIDTH + SW_WIDTH:
                t0, quarter = 6 * LANE, SW_HD // 4
            elif 3 * DA_WIDTH + SW_WIDTH <= c0 < 3 * DA_WIDTH + SW_WIDTH + SW_KV_HEADS * SW_HD:
                t0, quarter = 9 * LANE, SW_HD // 4
            else:
                t0 = None
            if t0 is not None:
                cos = rope_ref[rows, t0:t0 + LANE]
                sa = rope_ref[rows, t0 + LANE:t0 + 2 * LANE]
                sb = rope_ref[rows, t0 + 2 * LANE:t0 + 3 * LANE]
                acc = jnp.concatenate(
                    [_rope(acc[:, :LANE], cos, sa, sb, quarter), _rope(acc[:, LANE:], cos, sa, sb, quarter)], axis=1)
            p_ref[rows, c0:c0 + 256] = acc.astype(BF16)


def _k1(xall, mods4, g1, w_in, layer, rope, n_lat_tiles):
    b, s_pad = xall.shape[0], rope.shape[0]
    _, d, ncols = w_in.shape

    def mrow(k):
        return lambda bi, i: (jnp.where(i >= n_lat_tiles, 8, bi), k, 0, 0)

    return pl.pallas_call(
        _k1_kernel,
        grid=(b, s_pad // TM),
        in_specs=[
            pl.BlockSpec((None, TM, d), lambda bi, i: (bi, i, 0)),
            pl.BlockSpec((None, None, 1, d), mrow(0)),
            pl.BlockSpec((None, None, 1, d), mrow(1)),
            pl.BlockSpec((1, d), lambda bi, i: (0, 0)),
            pl.BlockSpec((None, d, ncols), lambda bi, i: (layer, 0, 0), pipeline_mode=pl.Buffered(1)),
            pl.BlockSpec((TM, 12 * LANE), lambda bi, i: (i, 0)),
        ],
        out_specs=[
            pl.BlockSpec((None, TM, ATT_COLS), lambda bi, i: (bi, i, 0)),
            pl.BlockSpec((TM, SSM_WIDTH), lambda bi, i: (i, bi)),
        ],
        out_shape=[
            jax.ShapeDtypeStruct((b, s_pad, ATT_COLS), BF16),
            jax.ShapeDtypeStruct((s_pad, b * SSM_WIDTH), F32),
        ],
        compiler_params=_cparams(("parallel", "parallel")),
        name="norm_inproj",
    )(xall, mods4, mods4, g1, w_in, rope)


def _exp2_parts(s):
    m = jnp.max(s, axis=-1, keepdims=True)
    e = jnp.exp2(s - m)
    return e, jnp.sum(e, axis=-1, keepdims=True)


def _da_kernel(q_ref, k_ref, v_ref, lp_ref, g_ref, o_ref, *, n_lat_q, l_lat, l_ctx, lambda_init):
    i = pl.program_id(2)
    lp = lp_ref[...]
    lam = (jnp.exp(jnp.sum(lp[0:1] * lp[1:2], axis=-1, keepdims=True))
           - jnp.exp(jnp.sum(lp[2:3] * lp[3:4], axis=-1, keepdims=True)) + lambda_init)

    def attend(k, v):
        q = q_ref[...]
        lane = lax.broadcasted_iota(jnp.int32, q.shape, 1)
        zero = jnp.zeros_like(q)
        dn = (((1,), (1,)), ((), ()))
        e1, l1 = _exp2_parts(lax.dot_general(jnp.where(lane < DA_QK, q, zero), k, dn, preferred_element_type=F32))
        e2, l2 = _exp2_parts(lax.dot_general(jnp.where(lane >= DA_QK, q, zero), k, dn, preferred_element_type=F32))
        o1 = jnp.dot(e1.astype(BF16), v, preferred_element_type=F32)
        o2 = jnp.dot(e2.astype(BF16), v, preferred_element_type=F32)
        o = o1 * (1.0 / l1) - o2 * (lam / l2)
        o = o * lax.rsqrt(jnp.mean(o * o, axis=-1, keepdims=True) + NORM_EPS) * g_ref[...]
        o_ref[...] = (o * (1.0 - lambda_init)).astype(BF16)

    @pl.when(i < n_lat_q)
    def _():
        attend(k_ref[0:l_lat + l_ctx, :], v_ref[0:l_lat + l_ctx, :])

    @pl.when(i == n_lat_q)
    def _():
        attend(k_ref[l_lat:l_lat + l_ctx, :], v_ref[l_lat:l_lat + l_ctx, :])

    @pl.when(i > n_lat_q)
    def _():
        o_ref[...] = jnp.zeros_like(o_ref)


def _diff_attention(p, lam_params, subln_g, l_lat, l_ctx, lambda_init):
    b, s_pad, _ = p.shape
    kern = functools.partial(_da_kernel, n_lat_q=l_lat // TQ, l_lat=l_lat, l_ctx=l_ctx, lambda_init=lambda_init)
    return pl.pallas_call(
        kern,
        grid=(b, DA_HEADS, s_pad // TQ),
        in_specs=[
            pl.BlockSpec((None, TQ, LANE), lambda bi, h, i: (bi, i, h)),
            pl.BlockSpec((None, s_pad, LANE), lambda bi, h, i: (bi, 0, DA_HEADS + h)),
            pl.BlockSpec((None, s_pad, LANE), lambda bi, h, i: (bi, 0, 2 * DA_HEADS + h)),
            pl.BlockSpec((8, LANE), lambda bi, h, i: (0, 0)),
            pl.BlockSpec((1, LANE), lambda bi, h, i: (0, 0)),
        ],
        out_specs=pl.BlockSpec((None, TQ, LANE), lambda bi, h, i: (bi, i, h)),
        out_shape=jax.ShapeDtypeStruct((b, s_pad, DA_WIDTH), BF16),
        compiler_params=_cparams(("parallel", "parallel", "arbitrary")),
        name="diff_attention",
    )(p, p, p, lam_params, subln_g)


def _sw_kernel(sink_ref, q_ref, k_ref, v_ref, o_ref, *, n_lat_q, l_lat, l_ctx):
    i = pl.program_id(2)
    kvh = pl.program_id(1)
    dn = (((1,), (1,)), ((), ()))
    band_w = TQ + 2 * SW_WINDOW
    kc = k_ref[l_lat:l_lat + l_ctx, :]
    vc = v_ref[l_lat:l_lat + l_ctx, :]

    def ctx_scores(qg):
        return lax.dot_general(qg, kc, dn, preferred_element_type=F32)

    @pl.when(i < n_lat_q)
    def _():
        q0 = i * TQ
        kstart = pl.multiple_of(jnp.clip(q0 - SW_WINDOW, 0, l_lat - band_w), SW_WINDOW)
        kb = k_ref[pl.ds(kstart, band_w), :]
        vb = v_ref[pl.ds(kstart, band_w), :]
        qpos = q0 + lax.broadcasted_iota(jnp.int32, (TQ, band_w), 0)
        kpos = kstart + lax.broadcasted_iota(jnp.int32, (TQ, band_w), 1)
        in_band = jnp.abs(qpos - kpos) <= SW_WINDOW
        for g in range(SW_GROUP):
            qg = q_ref[:, g * SW_HD:(g + 1) * SW_HD]
            sink = sink_ref[kvh * SW_GROUP + g] * LOG2_E
            sb = jnp.where(in_band, lax.dot_general(qg, kb, dn, preferred_element_type=F32), NEG_INF)
            sc = ctx_scores(qg)
            m = jnp.maximum(jnp.maximum(jnp.max(sb, axis=-1, keepdims=True), jnp.max(sc, axis=-1, keepdims=True)), sink)
            eb = jnp.exp2(sb - m)
            ec = jnp.exp2(sc - m)
            denom = jnp.sum(eb, axis=-1, keepdims=True) + jnp.sum(ec, axis=-1, keepdims=True) + jnp.exp2(sink - m)
            o = (jnp.dot(eb.astype(BF16), vb, preferred_element_type=F32)
                 + jnp.dot(ec.astype(BF16), vc, preferred_element_type=F32))
            o_ref[:, g * SW_HD:(g + 1) * SW_HD] = (o * (1.0 / denom)).astype(BF16)

    @pl.when(i == n_lat_q)
    def _():
        for g in range(SW_GROUP):
            qg = q_ref[:, g * SW_HD:(g + 1) * SW_HD]
            sink = sink_ref[kvh * SW_GROUP + g] * LOG2_E
            sc = ctx_scores(qg)
            m = jnp.maximum(jnp.max(sc, axis=-1, keepdims=True), sink)
            ec = jnp.exp2(sc - m)
            denom = jnp.sum(ec, axis=-1, keepdims=True) + jnp.exp2(sink - m)
            o = jnp.dot(ec.astype(BF16), vc, preferred_element_type=F32)
            o_ref[:, g * SW_HD:(g + 1) * SW_HD] = (o * (1.0 / denom)).astype(BF16)

    @pl.when(i > n_lat_q)
    def _():
        o_ref[...] = jnp.zeros_like(o_ref)


def _window_attention(p, sink, l_lat, l_ctx):
    b, s_pad, _ = p.shape
    gw = SW_GROUP * SW_HD
    q_blk0 = 3 * DA_WIDTH // gw
    k_blk0 = (3 * DA_WIDTH + SW_WIDTH) // SW_HD
    v_blk0 = k_blk0 + SW_KV_HEADS
    kern = functools.partial(_sw_kernel, n_lat_q=l_lat // TQ, l_lat=l_lat, l_ctx=l_ctx)
    return pl.pallas_call(
        kern,
        grid=(b, SW_KV_HEADS, s_pad // TQ),
        in_specs=[
            pl.BlockSpec(memory_space=pltpu.SMEM),
            pl.BlockSpec((None, TQ, gw), lambda bi, h, i: (bi, i, q_blk0 + h)),
            pl.BlockSpec((None, s_pad, SW_HD), lambda bi, h, i: (bi, 0, k_blk0 + h)),
            pl.BlockSpec((None, s_pad, SW_HD), lambda bi, h, i: (bi, 0, v_blk0 + h)),
        ],
        out_specs=pl.BlockSpec((None, TQ, gw), lambda bi, h, i: (bi, i, h)),
        out_shape=jax.ShapeDtypeStruct((b, s_pad, SW_WIDTH), BF16),
        compiler_params=_cparams(("parallel", "parallel", "arbitrary")),
        name="window_attention",
    )(sink, p, p, p)


def _s5_discretise(lam_re, lam_im, log_dt, b_re, b_im):
    lr = jnp.minimum(lam_re.astype(F32), SSM_RE_MAX)
    li = lam_im.astype(F32)
    dt = jnp.exp(log_dt.astype(F32))[..., None]
    mag = jnp.exp(lr * dt)
    lbr = mag * jnp.cos(li * dt)
    lbi = mag * jnp.sin(li * dt)
    den = lr * lr + li * li
    cr = ((lbr - 1.0) * lr + lbi * li) / den
    ci = (lbi * lr - (lbr - 1.0) * li) / den
    br = b_re.astype(F32)
    bi = b_im.astype(F32)
    bbr = cr[..., None] * br - ci[..., None] * bi
    bbi = cr[..., None] * bi + ci[..., None] * br
    return lbr, lbi, bbr, bbi


def _s5_matrices(lam_re, lam_im, log_dt, b_re, b_im, c_re, c_im):
    lbr, lbi, bbr, bbi = _s5_discretise(lam_re, lam_im, log_dt, b_re, b_im)
    eye = jnp.eye(SSM_GROUPS, dtype=F32)

    def in_map(bb):
        return jnp.einsum("dgnp,gh->dgphn", bb, eye).reshape(2, SSM_WIDTH, SSM_LANES)

    def out_map(cc):
        return jnp.einsum("dgpn,gh->dgnhp", cc, eye).reshape(2, SSM_LANES, SSM_WIDTH)

    b_blk = jnp.concatenate([in_map(bbr), in_map(bbi)], axis=2).astype(BF16)
    c_blk = jnp.concatenate([out_map(c_re.astype(F32)), out_map(-c_im.astype(F32))], axis=1).astype(BF16)
    lam_b = jnp.concatenate([lbr.reshape(2, 1, SSM_LANES), lbi.reshape(2, 1, SSM_LANES)], axis=2)
    lam_b = jnp.broadcast_to(lam_b, (2, 8, 2 * SSM_LANES))
    return b_blk, c_blk, lam_b


def _s5_kernel(u_ref, b_ref, c_ref, lam_ref, y_ref, bu_ref, hs_ref, h_ref, *, n_batch):
    d = pl.program_id(0)
    s = pl.program_id(1)
    rows = TT * n_batch

    @pl.when(s == 0)
    def _():
        h_ref[...] = jnp.zeros_like(h_ref)

    u = u_ref[...].astype(BF16)
    for c in range(2 * SSM_LANES // 256):
        k0 = ((c % (SSM_LANES // 256)) * 256 // SSM_STATE * SSM_GROUP_CH) // LANE * LANE
        bu_ref[:, c * 256:(c + 1) * 256] = jnp.dot(
            u[:, k0:k0 + LANE], b_ref[k0:k0 + LANE, c * 256:(c + 1) * 256], preferred_element_type=F32)

    cw = 512
    for c in range(SSM_LANES // cw):
        re = slice(c * cw, (c + 1) * cw)
        im = slice(SSM_LANES + c * cw, SSM_LANES + (c + 1) * cw)
        lr = lam_ref[:, re]
        li = lam_ref[:, im]

        def step(t, carry, re=re, im=im, lr=lr, li=li):
            hr, hi = carry
            tt = jnp.where(d == 0, t, TT - 1 - t)
            r = pl.multiple_of(tt * n_batch, n_batch)
            nr = lr * hr - li * hi + bu_ref[pl.ds(r, n_batch), re]
            ni = lr * hi + li * hr + bu_ref[pl.ds(r, n_batch), im]
            hs_ref[pl.ds(r, n_batch), re] = nr
            hs_ref[pl.ds(r, n_batch), im] = ni
            return nr, ni

        hr, hi = lax.fori_loop(0, TT, step, (h_ref[:, re], h_ref[:, im]), unroll=4)
        h_ref[:, re] = hr
        h_ref[:, im] = hi

    half = SSM_LANES // 2
    for j in range(SSM_WIDTH // 256):
        acc = jnp.dot(hs_ref[:, j * half:(j + 1) * half].astype(BF16),
                      c_ref[j * half:(j + 1) * half, j * 256:(j + 1) * 256], preferred_element_type=F32)
        acc += jnp.dot(hs_ref[:, SSM_LANES + j * half:SSM_LANES + (j + 1) * half].astype(BF16),
                       c_ref[SSM_LANES + j * half:SSM_LANES + (j + 1) * half, j * 256:(j + 1) * 256],
                       preferred_element_type=F32)
        y_ref[:, j * 256:(j + 1) * 256] = acc


def _s5_scan(u_rows, b_blk, c_blk, lam_b, n_batch, l_lat, l_ctx):
    rows = TT * n_batch
    n_lat, n_ctx = l_lat // TT, l_ctx // TT
    n_steps = n_lat + n_ctx

    def tile(d, s):
        fwd = jnp.where(s < n_ctx, n_lat + s, s - n_ctx)
        bwd = jnp.where(s < n_ctx, n_lat + n_ctx - 1 - s, n_lat - 1 - (s - n_ctx))
        return jnp.where(d == 0, fwd, bwd)

    kern = functools.partial(_s5_kernel, n_batch=n_batch)
    return pl.pallas_call(
        kern,
        grid=(2, n_steps),
        in_specs=[
            pl.BlockSpec((rows, SSM_WIDTH), lambda d, s: (tile(d, s), 0)),
            pl.BlockSpec((None, SSM_WIDTH, 2 * SSM_LANES), lambda d, s: (d, 0, 0)),
            pl.BlockSpec((None, 2 * SSM_LANES, SSM_WIDTH), lambda d, s: (d, 0, 0)),
            pl.BlockSpec((None, 8, 2 * SSM_LANES), lambda d, s: (d, 0, 0)),
        ],
        out_specs=pl.BlockSpec((None, rows, SSM_WIDTH), lambda d, s: (d, tile(d, s), 0)),
        out_shape=jax.ShapeDtypeStruct((2, (l_lat + l_ctx) * n_batch, SSM_WIDTH), F32),
        scratch_shapes=[
            pltpu.VMEM((rows, 2 * SSM_LANES), F32),
            pltpu.VMEM((rows, 2 * SSM_LANES), F32),
            pltpu.VMEM((n_batch, 2 * SSM_LANES), F32),
        ],
        compiler_params=_cparams(("arbitrary", "arbitrary")),
        name="s5_scan",
    )(u_rows, b_blk, c_blk, lam_b)


def _glu_kernel(u_ref, y_ref, d_ref, w_ref, o_ref, *, n_valid_tiles):
    @pl.when(pl.program_id(0) < n_valid_tiles)
    def _():
        y = u_ref[...] * d_ref[...] + y_ref[0] + y_ref[1]
        g = jax.nn.gelu(y)
        z = jnp.dot(g.astype(BF16), w_ref[...], preferred_element_type=F32)
        o_ref[...] = (g * jax.nn.sigmoid(z)).astype(BF16)

    @pl.when(pl.program_id(0) >= n_valid_tiles)
    def _():
        o_ref[...] = jnp.zeros_like(o_ref)


def _s5_glu(u_rows, y, d_skip, w_glu, n_valid_rows):
    n_rows = u_rows.shape[0]
    tr = 512
    n_valid_tiles = n_valid_rows // tr
    return pl.pallas_call(
        functools.partial(_glu_kernel, n_valid_tiles=n_valid_tiles),
        grid=(n_rows // tr,),
        in_specs=[
            pl.BlockSpec((tr, SSM_WIDTH), lambda i: (i, 0)),
            pl.BlockSpec((2, tr, SSM_WIDTH), lambda i: (0, jnp.minimum(i, n_valid_tiles - 1), 0)),
            pl.BlockSpec((1, SSM_WIDTH), lambda i: (0, 0)),
            pl.BlockSpec((SSM_WIDTH, SSM_WIDTH), lambda i: (0, 0)),
        ],
        out_specs=pl.BlockSpec((tr, SSM_WIDTH), lambda i: (i, 0)),
        out_shape=jax.ShapeDtypeStruct((n_rows, SSM_WIDTH), BF16),
        compiler_params=_cparams(("parallel",)),
        name="s5_glu",
    )(u_rows, y, d_skip, w_glu)


def _pack_bf16_pairs(h):
    n = h.shape[1] // 2
    bits = pltpu.bitcast(h.astype(BF16).astype(F32), jnp.uint32)
    return pltpu.bitcast(bits[:, :n] | (bits[:, n:] >> 16), F32)


def _unpack_bf16_pairs(words):
    bits = pltpu.bitcast(words, jnp.uint32)
    hi = pltpu.bitcast(bits & jnp.uint32(0xFFFF0000), F32).astype(BF16)
    lo = pltpu.bitcast(bits << 16, F32).astype(BF16)
    return hi, lo


def _kout_kernel(da_ref, sw_ref, ss_ref, x_ref, gate_ref, sh_ref, sc_ref, g_ref, w_ref, wr_ref, xh_ref, lg_ref):
    d = g_ref.shape[-1]
    hr = TM // ROW_SPLIT
    for r in range(ROW_SPLIT):
        rows = slice(r * hr, (r + 1) * hr)
        acc = jnp.dot(da_ref[rows, :], w_ref[0:DA_WIDTH, :], preferred_element_type=F32)
        acc += jnp.dot(sw_ref[rows, :], w_ref[DA_WIDTH:DA_WIDTH + SW_WIDTH, :], preferred_element_type=F32)
        acc += jnp.dot(ss_ref[rows, :], w_ref[DA_WIDTH + SW_WIDTH:, :], preferred_element_type=F32)
        x1 = x_ref[rows, :] + gate_ref[...] * acc
        xh_ref[rows, 0:d] = x1
        y = x1 * lax.rsqrt(jnp.mean(x1 * x1, axis=-1, keepdims=True) + NORM_EPS) * g_ref[...]
        h2 = y * (1.0 + sc_ref[...]) + sh_ref[...]
        xh_ref[rows, d:d + d // 2] = _pack_bf16_pairs(h2)
        lg_ref[rows, :] = jnp.dot(h2.astype(BF16), wr_ref[...], preferred_element_type=F32)


def _kout(da, sw, ss, xall, mods4, g2, w_out, layer, w_router, n_lat_tiles):
    b, s_pad = da.shape[0], da.shape[1]
    d = w_out.shape[2]

    def mrow(k):
        return lambda bi, i: (jnp.where(i >= n_lat_tiles, 8, bi), k, 0, 0)

    tok = lambda bi, i: (bi, i, 0)
    return pl.pallas_call(
        _kout_kernel,
        grid=(b, s_pad // TM),
        in_specs=[
            pl.BlockSpec((None, TM, DA_WIDTH), tok),
            pl.BlockSpec((None, TM, SW_WIDTH), tok),
            pl.BlockSpec((TM, SSM_WIDTH), lambda bi, i: (i, bi)),
            pl.BlockSpec((None, TM, d), tok),
            pl.BlockSpec((None, None, 1, d), mrow(2)),
            pl.BlockSpec((None, None, 1, d), mrow(3)),
            pl.BlockSpec((None, None, 1, d), mrow(4)),
            pl.BlockSpec((1, d), lambda bi, i: (0, 0)),
            pl.BlockSpec((None, d, d), lambda bi, i: (layer, 0, 0), pipeline_mode=pl.Buffered(1)),
            pl.BlockSpec((d, LANE), lambda bi, i: (0, 0)),
        ],
        out_specs=[
            pl.BlockSpec((None, TM, d + d // 2), tok),
            pl.BlockSpec((None, TM, LANE), tok),
        ],
        out_shape=[
            jax.ShapeDtypeStruct((b, s_pad, d + d // 2), F32),
            jax.ShapeDtypeStruct((b, s_pad, LANE), F32),
        ],
        compiler_params=_cparams(("parallel", "parallel")),
        name="outproj_norm_router",
    )(da, sw, ss, xall, mods4, mods4, mods4, g2, w_out, w_router)


def _lane_cumsum(x, tri):
    n = x.shape[1]
    total = jnp.zeros((x.shape[0], 1), F32)
    parts = []
    for c in range(n // 256):
        loc = jnp.dot(x[:, c * 256:(c + 1) * 256].astype(BF16), tri, preferred_element_type=F32)
        parts.append(loc + total)
        total = total + loc[:, 255:256]
    return jnp.concatenate(parts, axis=1)


def _route_kernel(lg_ref, tok_ref, idx_ref, gate_ref, *, n_tok, cap):
    lg = lg_ref[...]
    lane = lax.broadcasted_iota(jnp.int32, lg.shape, 1)
    lg = jnp.where(lane < N_EXPERTS, lg, NEG_INF)
    e = jnp.exp(lg - jnp.max(lg, axis=-1, keepdims=True))
    aff = e / jnp.sum(e, axis=-1, keepdims=True)
    aff_t = jnp.transpose(aff)[0:N_EXPERTS, :]
    bits = pltpu.bitcast(aff_t, jnp.int32)

    def search(k, thr):
        cand = thr | lax.shift_left(jnp.int32(1), 30 - k)
        cnt = jnp.sum((bits >= cand).astype(F32), axis=-1, keepdims=True)
        return jnp.where(cnt >= cap, cand, thr)

    thr = lax.fori_loop(0, 31, search, jnp.zeros((N_EXPERTS, 1), jnp.int32))
    gt = bits > thr
    eq = bits == thr
    r_i = lax.broadcasted_iota(jnp.int32, (256, 256), 0)
    c_i = lax.broadcasted_iota(jnp.int32, (256, 256), 1)
    tri = (r_i <= c_i).astype(BF16)
    need = cap - jnp.sum(gt.astype(F32), axis=-1, keepdims=True)
    eq_f = eq.astype(F32)
    eq_rank = _lane_cumsum(eq_f, tri) - eq_f
    sel = jnp.where(gt, 1.0, jnp.where(eq & (eq_rank < need), 1.0, 0.0))
    pos = _lane_cumsum(sel, tri) - sel
    pos = jnp.where(sel > 0.0, pos, -1.0)

    a_hi = aff.astype(BF16)
    r1 = aff - a_hi.astype(F32)
    a_mid = r1.astype(BF16)
    a_lo = (r1 - a_mid.astype(F32)).astype(BF16)
    tok = tok_ref[...]
    payload = jnp.where(lane < 16, a_hi.astype(F32),
                        jnp.where(lane < 32, pltpu.roll(a_mid.astype(F32), 16, 1),
                                  jnp.where(lane < 48, pltpu.roll(a_lo.astype(F32), 32, 1), tok))).astype(BF16)

    slot = lax.broadcasted_iota(jnp.int32, (cap, 256), 0).astype(F32)
    out_lane = lax.broadcasted_iota(jnp.int32, (cap, LANE), 1)
    for ex in range(N_EXPERTS):
        acc = jnp.zeros((cap, LANE), F32)
        for c in range(n_tok // 256):
            onehot = jnp.where(pos[ex:ex + 1, c * 256:(c + 1) * 256] == slot, 1.0, 0.0).astype(BF16)
            acc += jnp.dot(onehot, payload[c * 256:(c + 1) * 256, :], preferred_element_type=F32)
        pick = lambda l: jnp.sum(jnp.where(out_lane == l, acc, 0.0), axis=-1, keepdims=True)
        gate_ref[ex] = (pick(ex) + pick(16 + ex)) + pick(32 + ex)
        idx_ref[ex] = (pick(48) * 64.0 + pick(49)).astype(jnp.int32)


def _route(lg, tok_cols, row_block, n_tok):
    b = lg.shape[0]
    cap = EC_CAPACITY * n_tok // N_EXPERTS
    kern = functools.partial(_route_kernel, n_tok=n_tok, cap=cap)
    idx, gate = pl.pallas_call(
        kern,
        grid=(b,),
        in_specs=[
            pl.BlockSpec((None, n_tok, LANE), lambda bi: (bi, row_block, 0)),
            pl.BlockSpec((n_tok, LANE), lambda bi: (0, 0)),
        ],
        out_specs=[
            pl.BlockSpec((None, N_EXPERTS, cap, 1), lambda bi: (bi, 0, 0, 0)),
            pl.BlockSpec((None, N_EXPERTS, cap, 1), lambda bi: (bi, 0, 0, 0)),
        ],
        out_shape=[
            jax.ShapeDtypeStruct((b, N_EXPERTS, cap, 1), jnp.int32),
            jax.ShapeDtypeStruct((b, N_EXPERTS, cap, 1), F32),
        ],
        compiler_params=_cparams(("parallel",)),
        name="route",
    )(lg, tok_cols)
    return idx[..., 0], gate


def _token_columns(n_tok):
    t = jnp.arange(n_tok, dtype=jnp.int32)[:, None]
    lane = jnp.arange(LANE, dtype=jnp.int32)[None, :]
    return jnp.where(lane == 48, t // 64, jnp.where(lane == 49, t % 64, 0)).astype(F32)


ROW_SLOTS = 3
DMA_UNROLL = 8


def _ffn_kernel(idx_prev_ref, idx_ref, idx_next_ref, gate_ref, m5_ref, m5c_ref, wg_ref, wu_ref, wd_ref, xh_in_hbm,
                xh_hbm, rbuf, sem_in, sem_out, *, n_rows, n_lat_rows, n_batch):
    del xh_in_hbm
    d = wd_ref.shape[-1]
    ff = wg_ref.shape[-1]
    b = pl.program_id(1)
    s = pl.program_id(0) * n_batch + b
    n_steps = N_EXPERTS * n_batch
    slot = s % ROW_SLOTS
    next_slot = (s + 1) % ROW_SLOTS
    prev_slot = (s + 2) % ROW_SLOTS
    next_sample = (b + 1) % n_batch
    prev_sample = (b + n_batch - 1) % n_batch

    def gather_row(idx_r, sample, to_slot, c):
        return pltpu.make_async_copy(xh_hbm.at[sample, pl.ds(idx_r[0, c], 1), :], rbuf.at[to_slot, pl.ds(c, 1), :],
                                     sem_in.at[to_slot])

    def scatter_row(sample, t, from_slot, c):
        return pltpu.make_async_copy(rbuf.at[from_slot, pl.ds(c, 1), pl.ds(0, d)],
                                     xh_hbm.at[sample, pl.ds(t, 1), pl.ds(0, d)], sem_out.at[from_slot])

    def wait_rows_in(at_slot):
        pltpu.make_async_copy(xh_hbm.at[0, pl.ds(0, n_rows), :], rbuf.at[at_slot], sem_in.at[at_slot]).wait()

    def wait_rows_out(at_slot):
        pltpu.make_async_copy(rbuf.at[at_slot, :, pl.ds(0, d)], xh_hbm.at[0, pl.ds(0, n_rows), pl.ds(0, d)],
                              sem_out.at[at_slot]).wait()

    @pl.when(s == 0)
    def _():
        def body(c, carry):
            gather_row(idx_ref, b, slot, c).start()
            gather_row(idx_prev_ref, prev_sample, prev_slot, c).start()
            return carry

        lax.fori_loop(0, n_rows, body, 0, unroll=DMA_UNROLL)
        wait_rows_in(prev_slot)

    @pl.when(s >= 1)
    def _():
        wait_rows_out(next_slot)

    wait_rows_in(slot)

    fc = 256
    n_units = ff // fc + d // fc
    per_unit = -(-n_rows // n_units)
    issued = [0]

    def issue_row_copies():
        for c in range(issued[0], min(issued[0] + per_unit, n_rows)):
            gather_row(idx_next_ref, next_sample, next_slot, c).start()
            scatter_row(prev_sample, idx_prev_ref[0, c], prev_slot, c).start()
        issued[0] = min(issued[0] + per_unit, n_rows)

    x_hi, x_lo = _unpack_bf16_pairs(rbuf[slot, :, d:d + d // 2])
    hid = []
    for j in range(ff // fc):
        cols = slice(j * fc, (j + 1) * fc)
        hg = (jnp.dot(x_hi, wg_ref[0:d // 2, cols], preferred_element_type=F32)
              + jnp.dot(x_lo, wg_ref[d // 2:d, cols], preferred_element_type=F32))
        hu = (jnp.dot(x_hi, wu_ref[0:d // 2, cols], preferred_element_type=F32)
              + jnp.dot(x_lo, wu_ref[d // 2:d, cols], preferred_element_type=F32))
        hid.append((hg * jax.nn.sigmoid(hg) * hu).astype(BF16))
        issue_row_copies()
    hid = jnp.concatenate(hid, axis=1)
    row = lax.broadcasted_iota(jnp.int32, (n_rows, 1), 0)
    is_lat = row < n_lat_rows
    gate = gate_ref[...]
    for j in range(d // fc):
        cols = slice(j * fc, (j + 1) * fc)
        y = jnp.dot(hid, wd_ref[:, cols], preferred_element_type=F32)
        m5 = jnp.where(is_lat, m5_ref[:, cols], m5c_ref[:, cols])
        rbuf[slot, :, cols] = rbuf[slot, :, cols] + m5 * (y * gate)
        issue_row_copies()
    assert issued[0] == n_rows

    @pl.when(s == n_steps - 1)
    def _():
        wait_rows_in(next_slot)
        wait_rows_out(prev_slot)

        def body(c, carry):
            scatter_row(b, idx_ref[0, c], slot, c).start()
            return carry

        lax.fori_loop(0, n_rows, body, 0, unroll=DMA_UNROLL)
        wait_rows_out(slot)


def _expert_ffn(idx, gate, mods4, w_gate, w_up, w_down, layer, xh, n_lat_rows):
    b, n_xh_rows, row_w = xh.shape
    n_rows = idx.shape[2]
    _, _, d, ff = w_gate.shape
    kern = functools.partial(_ffn_kernel, n_rows=n_rows, n_lat_rows=n_lat_rows, n_batch=b)
    idx4 = idx[:, :, None, :]
    steps = jnp.swapaxes(idx, 0, 1).reshape(N_EXPERTS * b, n_rows)
    idx_prev4 = jnp.concatenate([steps[b - 1:b], steps[:-1]], axis=0).reshape(N_EXPERTS, b, 1, n_rows)

    def next_step(e, bi):
        return (bi + 1) % b, jnp.minimum(e + (bi + 1) // b, N_EXPERTS - 1), 0, 0

    return pl.pallas_call(
        kern,
        grid=(N_EXPERTS, b),
        in_specs=[
            pl.BlockSpec((None, None, 1, n_rows), lambda e, bi: (e, bi, 0, 0), memory_space=pltpu.SMEM),
            pl.BlockSpec((None, None, 1, n_rows), lambda e, bi: (bi, e, 0, 0), memory_space=pltpu.SMEM),
            pl.BlockSpec((None, None, 1, n_rows), next_step, memory_space=pltpu.SMEM),
            pl.BlockSpec((None, None, n_rows, 1), lambda e, bi: (bi, e, 0, 0)),
            pl.BlockSpec((None, None, 1, d), lambda e, bi: (bi, 5, 0, 0)),
            pl.BlockSpec((None, None, 1, d), lambda e, bi: (8, 5, 0, 0)),
            pl.BlockSpec((None, None, d, ff), lambda e, bi: (layer, e, 0, 0)),
            pl.BlockSpec((None, None, d, ff), lambda e, bi: (layer, e, 0, 0)),
            pl.BlockSpec((None, None, ff, d), lambda e, bi: (layer, e, 0, 0)),
            pl.BlockSpec(memory_space=pl.ANY),
        ],
        out_specs=pl.BlockSpec(memory_space=pl.ANY),
        out_shape=jax.ShapeDtypeStruct((b, n_xh_rows, row_w), F32),
        scratch_shapes=[
            pltpu.VMEM((ROW_SLOTS, n_rows, row_w), F32),
            pltpu.SemaphoreType.DMA((ROW_SLOTS,)),
            pltpu.SemaphoreType.DMA((ROW_SLOTS,)),
        ],
        input_output_aliases={9: 0},
        compiler_params=pltpu.CompilerParams(
            dimension_semantics=("arbitrary", "arbitrary"), vmem_limit_bytes=VMEM_LIMIT, has_side_effects=True),
        name="expert_ffn",
    )(idx_prev4, idx4, idx4, gate, mods4, mods4, w_gate, w_up, w_down, xh)


def _final_kernel(x_ref, g_ref, o_ref):
    x = x_ref[...]
    o_ref[...] = x * lax.rsqrt(jnp.mean(x * x, axis=-1, keepdims=True) + NORM_EPS) * g_ref[...]


def _final_norm(xall, g, l_lat):
    b, d = xall.shape[0], g.shape[-1]
    return pl.pallas_call(
        _final_kernel,
        grid=(b, l_lat // TM),
        in_specs=[pl.BlockSpec((None, TM, d), lambda bi, i: (bi, i, 0)), pl.BlockSpec((1, d), lambda bi, i: (0, 0))],
        out_specs=pl.BlockSpec((None, TM, d), lambda bi, i: (bi, i, 0)),
        out_shape=jax.ShapeDtypeStruct((b, l_lat, d), F32),
        compiler_params=_cparams(("parallel", "parallel")),
        name="final_norm",
    )(xall, g)


def _rope_table(l_lat, s_pad, head_dim):
    q = head_dim // 4
    inv_freq = ROPE_THETA ** (-jnp.arange(q, dtype=F32) / q)
    pos = jnp.arange(l_lat)
    ang_r = (pos // GRID_W).astype(F32)[:, None] * inv_freq[None, :]
    ang_c = (pos % GRID_W).astype(F32)[:, None] * inv_freq[None, :]
    ang = jnp.concatenate([ang_r, ang_r, ang_c, ang_c], axis=-1)
    ang = jnp.tile(ang, (1, LANE // head_dim))
    lane = jnp.arange(LANE)[None, :]
    first = (lane % (2 * q)) < q
    cos, sin = jnp.cos(ang), jnp.sin(ang)
    tab = jnp.concatenate([cos, jnp.where(first, -sin, 0.0), jnp.where(first, 0.0, sin)], axis=-1)
    ident = jnp.concatenate([jnp.ones((s_pad - l_lat, LANE), F32), jnp.zeros((s_pad - l_lat, 2 * LANE), F32)], axis=-1)
    return jnp.concatenate([tab, ident], axis=0)


def kernel(x, c, ctx, c_ctx, w_mod, b_mod, norm1_g, norm2_g, w_in, w_out, da_lam_q1, da_lam_k1, da_lam_q2, da_lam_k2, da_subln_g, sw_sink, ssm_lam_re, ssm_lam_im, ssm_log_dt, ssm_b_re, ssm_b_im, ssm_c_re, ssm_c_im, ssm_d, ssm_w_glu, w_router, w_gate, w_up, w_down, final_g):
    b, l_lat, d = x.shape
    l_ctx = ctx.shape[1]
    depth = w_mod.shape[0]
    ctx_pad = -(-l_ctx // TM) * TM
    s_pad = l_lat + ctx_pad
    n_lat_tiles = l_lat // TM
    assert b == 8 and l_lat % TM == 0 and l_ctx % TQ == 0 and l_lat % GRID_W == 0

    xall = jnp.concatenate([x, ctx, jnp.zeros((b, ctx_pad - l_ctx, d), F32)], axis=1)
    craw = jnp.concatenate([c, c_ctx[None, :], jnp.zeros((16 - b - 1, d), F32)], axis=0)
    mods = _mods(craw, w_mod, b_mod)
    rope_da = _rope_table(l_lat, s_pad, DA_QK)
    rope_sw = _rope_table(l_lat, s_pad, SW_HD)
    rope = jnp.concatenate([rope_da * DA_Q_SCALE, rope_da, rope_sw * SW_Q_SCALE, rope_sw], axis=-1)
    tok_lat = _token_columns(l_lat)
    tok_ctx = _token_columns(l_ctx)
    w_in_b, w_out_b = w_in.astype(BF16), w_out.astype(BF16)
    w_gate_b, w_up_b, w_down_b = w_gate.astype(BF16), w_up.astype(BF16), w_down.astype(BF16)

    for l in range(depth):
        last = l == depth - 1
        lambda_init = 0.8 - 0.6 * math.exp(-0.3 * l)
        mods4 = mods[l].reshape(16, N_MOD, 1, d)
        p, u = _k1(xall, mods4, norm1_g[l][None, :], w_in_b, l, rope, n_lat_tiles)

        lam_params = jnp.zeros((8, LANE), F32).at[0:4, 0:DA_QK].set(
            jnp.stack([da_lam_q1[l], da_lam_k1[l], da_lam_q2[l], da_lam_k2[l]]).astype(F32))
        da = _diff_attention(p, lam_params, da_subln_g[l][None, :].astype(F32), l_lat, l_ctx, lambda_init)
        sw = _window_attention(p, sw_sink[l].astype(F32), l_lat, l_ctx)

        b_blk, c_blk, lam_b = _s5_matrices(ssm_lam_re[l], ssm_lam_im[l], ssm_log_dt[l], ssm_b_re[l], ssm_b_im[l],
                                           ssm_c_re[l], ssm_c_im[l])
        u_rows = u.reshape(s_pad * b, SSM_WIDTH)
        y = _s5_scan(u_rows, b_blk, c_blk, lam_b, b, l_lat, l_ctx)
        ss = _s5_glu(u_rows, y, ssm_d[l][None, :].astype(F32), ssm_w_glu[l].astype(BF16), (l_lat + l_ctx) * b)
        ss = ss.reshape(s_pad, b * SSM_WIDTH)

        w_r = jnp.zeros((d, LANE), BF16).at[:, 0:N_EXPERTS].set(w_router[l].astype(BF16))
        xh, lg = _kout(da, sw, ss, xall, mods4, norm2_g[l][None, :], w_out_b, l, w_r, n_lat_tiles)

        idx, gate = _route(lg, tok_lat, 0, l_lat)
        if not last:
            idx_c, gate_c = _route(lg, tok_ctx, l_lat // l_ctx, l_ctx)
            idx = jnp.concatenate([idx, idx_c + l_lat], axis=2)
            gate = jnp.concatenate([gate, gate_c], axis=2)
        xall = _expert_ffn(idx, gate, mods4, w_gate_b, w_up_b, w_down_b, l, xh, EC_CAPACITY * l_lat // N_EXPERTS)

    return _final_norm(xall, final_g[None, :], l_lat)
```

```python
import functools
import math

import jax
import jax.numpy as jnp
from jax import lax
from jax.experimental import pallas as pl
from jax.experimental.pallas import tpu as pltpu

F32 = jnp.float32
BF16 = jnp.bfloat16

GRID_W = 64
ROPE_THETA = 10000.0
NORM_EPS = 1e-6
NEG_INF = -1e30

DA_HEADS = 6
DA_QK = 64
DA_V = 128
DA_WIDTH = 768
SW_HEADS = 6
SW_KV_HEADS = 2
SW_GROUP = SW_HEADS // SW_KV_HEADS
SW_HD = 128
SW_WINDOW = 128
SW_WIDTH = 768
SSM_GROUPS = 32
SSM_GROUP_CH = 16
SSM_STATE = 64
SSM_WIDTH = 512
SSM_LANES = SSM_GROUPS * SSM_STATE
SSM_RE_MAX = -1e-4
ATT_COLS = 3 * DA_WIDTH + SW_WIDTH + 2 * SW_KV_HEADS * SW_HD
N_EXPERTS = 16
EC_CAPACITY = 2
N_MOD = 6

LOG2_E = math.log2(math.e)
DA_Q_SCALE = DA_QK ** -0.5 * LOG2_E
SW_Q_SCALE = SW_HD ** -0.5 * LOG2_E

TM = 512
TQ = 256
DA_TQ = 512
TT = 64
LANE = 128
VMEM_LIMIT = 56 * 1024 * 1024


def _cparams(sem):
    return pltpu.CompilerParams(dimension_semantics=sem, vmem_limit_bytes=VMEM_LIMIT)


def _mods_kernel(c_ref, w_ref, b_ref, o_ref):
    c = c_ref[...]
    s = (c * jax.nn.sigmoid(c)).astype(BF16)
    o_ref[...] = jnp.dot(s, w_ref[...].astype(BF16), preferred_element_type=F32) + b_ref[...]


def _mods(craw, w_mod, b_mod):
    depth, d, n = w_mod.shape
    tn = 1024
    return pl.pallas_call(
        _mods_kernel,
        grid=(depth, n // tn),
        in_specs=[
            pl.BlockSpec((16, d), lambda l, j: (0, 0)),
            pl.BlockSpec((None, d, tn), lambda l, j: (l, 0, j)),
            pl.BlockSpec((None, 1, tn), lambda l, j: (l, 0, j)),
        ],
        out_specs=pl.BlockSpec((None, 16, tn), lambda l, j: (l, 0, j)),
        out_shape=jax.ShapeDtypeStruct((depth, 16, n), F32),
        compiler_params=_cparams(("parallel", "parallel")),
        name="mods",
    )(craw, w_mod, b_mod.reshape(depth, 1, n))


def _rope(acc, cos, sa, sb, quarter):
    return (acc * cos + pltpu.roll(acc, LANE - quarter, 1) * sa + pltpu.roll(acc, quarter, 1) * sb)


ROW_SPLIT = 2


def _k1_kernel(x_ref, sh_ref, sc_ref, g_ref, w_ref, rope_ref, p_ref, u_ref):
    n_chunks = (ATT_COLS + SSM_WIDTH) // 256
    hr = TM // ROW_SPLIT
    for r in range(ROW_SPLIT):
        rows = slice(r * hr, (r + 1) * hr)
        x = x_ref[rows, :]
        y = x * lax.rsqrt(jnp.mean(x * x, axis=-1, keepdims=True) + NORM_EPS) * g_ref[...]
        h = (y * (1.0 + sc_ref[...]) + sh_ref[...]).astype(BF16)
        for j in range(n_chunks):
            c0 = j * 256
            acc = jnp.dot(h, w_ref[:, c0:c0 + 256], preferred_element_type=F32)
            if c0 >= ATT_COLS:
                u_ref[rows, c0 - ATT_COLS:c0 - ATT_COLS + 256] = acc
                continue
            if c0 < DA_WIDTH:
                t0, quarter = 0, DA_QK // 4
            elif c0 < 2 * DA_WIDTH:
                t0, quarter = 3 * LANE, DA_QK // 4
            elif 3 * DA_WIDTH <= c0 < 3 * DA_WIDTH + SW_WIDTH:
                t0, quarter = 6 * LANE, SW_HD // 4
            elif 3 * DA_WIDTH + SW_WIDTH <= c0 < 3 * DA_WIDTH + SW_WIDTH + SW_KV_HEADS * SW_HD:
                t0, quarter = 9 * LANE, SW_HD // 4
            else:
                t0 = None
            if t0 is not None:
                cos = rope_ref[rows, t0:t0 + LANE]
                sa = rope_ref[rows, t0 + LANE:t0 + 2 * LANE]
                sb = rope_ref[rows, t0 + 2 * LANE:t0 + 3 * LANE]
                acc = jnp.concatenate(
                    [_rope(acc[:, :LANE], cos, sa, sb, quarter), _rope(acc[:, LANE:], cos, sa, sb, quarter)], axis=1)
            p_ref[rows, c0:c0 + 256] = acc.astype(BF16)


def _k1(xall, mods4, g1, w_in, layer, rope, n_lat_tiles):
    b, s_pad = xall.shape[0], rope.shape[0]
    _, d, ncols = w_in.shape

    def mrow(k):
        return lambda bi, i: (jnp.where(i >= n_lat_tiles, 8, bi), k, 0, 0)

    return pl.pallas_call(
        _k1_kernel,
        grid=(b, s_pad // TM),
        in_specs=[
            pl.BlockSpec((None, TM, d), lambda bi, i: (bi, i, 0)),
            pl.BlockSpec((None, None, 1, d), mrow(0)),
            pl.BlockSpec((None, None, 1, d), mrow(1)),
            pl.BlockSpec((1, d), lambda bi, i: (0, 0)),
            pl.BlockSpec((None, d, ncols), lambda bi, i: (layer, 0, 0), pipeline_mode=pl.Buffered(1)),
            pl.BlockSpec((TM, 12 * LANE), lambda bi, i: (i, 0)),
        ],
        out_specs=[
            pl.BlockSpec((None, TM, ATT_COLS), lambda bi, i: (bi, i, 0)),
            pl.BlockSpec((TM, SSM_WIDTH), lambda bi, i: (i, bi)),
        ],
        out_shape=[
            jax.ShapeDtypeStruct((b, s_pad, ATT_COLS), BF16),
            jax.ShapeDtypeStruct((s_pad, b * SSM_WIDTH), F32),
        ],
        compiler_params=_cparams(("parallel", "parallel")),
        name="norm_inproj",
    )(xall, mods4, mods4, g1, w_in, rope)


def _exp2_parts(s):
    m = jnp.max(s, axis=-1, keepdims=True)
    total = jnp.zeros_like(m)
    chunks = []
    for c in range(0, s.shape[1], 256):
        e = jnp.exp2(s[:, c:c + 256] - m)
        total = total + jnp.sum(e, axis=-1, keepdims=True)
        chunks.append(e.astype(BF16))
    return jnp.concatenate(chunks, axis=1), total


def _da_kernel(q_ref, k_ref, v_ref, lp_ref, g_ref, o_ref, *, n_lat_q, l_lat, l_ctx, lambda_init):
    i = pl.program_id(2)
    lp = lp_ref[...]
    lam = (jnp.exp(jnp.sum(lp[0:1] * lp[1:2], axis=-1, keepdims=True))
           - jnp.exp(jnp.sum(lp[2:3] * lp[3:4], axis=-1, keepdims=True)) + lambda_init)

    def attend(k, v):
        q = q_ref[...]
        lane = lax.broadcasted_iota(jnp.int32, q.shape, 1)
        zero = jnp.zeros_like(q)
        dn = (((1,), (1,)), ((), ()))
        e1, l1 = _exp2_parts(lax.dot_general(jnp.where(lane < DA_QK, q, zero), k, dn, preferred_element_type=F32))
        e2, l2 = _exp2_parts(lax.dot_general(jnp.where(lane >= DA_QK, q, zero), k, dn, preferred_element_type=F32))
        o1 = jnp.dot(e1.astype(BF16), v, preferred_element_type=F32)
        o2 = jnp.dot(e2.astype(BF16), v, preferred_element_type=F32)
        o = o1 * (1.0 / l1) - o2 * (lam / l2)
        o = o * lax.rsqrt(jnp.mean(o * o, axis=-1, keepdims=True) + NORM_EPS) * g_ref[...]
        o_ref[...] = (o * (1.0 - lambda_init)).astype(BF16)

    @pl.when(i < n_lat_q)
    def _():
        attend(k_ref[0:l_lat + l_ctx, :], v_ref[0:l_lat + l_ctx, :])

    @pl.when(i == n_lat_q)
    def _():
        attend(k_ref[l_lat:l_lat + l_ctx, :], v_ref[l_lat:l_lat + l_ctx, :])

    @pl.when(i > n_lat_q)
    def _():
        o_ref[...] = jnp.zeros_like(o_ref)


def _diff_attention(p, lam_params, subln_g, l_lat, l_ctx, lambda_init):
    b, s_pad, _ = p.shape
    kern = functools.partial(_da_kernel, n_lat_q=l_lat // DA_TQ, l_lat=l_lat, l_ctx=l_ctx, lambda_init=lambda_init)
    return pl.pallas_call(
        kern,
        grid=(b, DA_HEADS, s_pad // DA_TQ),
        in_specs=[
            pl.BlockSpec((None, DA_TQ, LANE), lambda bi, h, i: (bi, i, h)),
            pl.BlockSpec((None, s_pad, LANE), lambda bi, h, i: (bi, 0, DA_HEADS + h)),
            pl.BlockSpec((None, s_pad, LANE), lambda bi, h, i: (bi, 0, 2 * DA_HEADS + h)),
            pl.BlockSpec((8, LANE), lambda bi, h, i: (0, 0)),
            pl.BlockSpec((1, LANE), lambda bi, h, i: (0, 0)),
        ],
        out_specs=pl.BlockSpec((None, DA_TQ, LANE), lambda bi, h, i: (bi, i, h)),
        out_shape=jax.ShapeDtypeStruct((b, s_pad, DA_WIDTH), BF16),
        compiler_params=_cparams(("parallel", "parallel", "arbitrary")),
        name="diff_attention",
    )(p, p, p, lam_params, subln_g)


def _sw_kernel(sink_ref, q_ref, k_ref, v_ref, o_ref, *, n_lat_q, l_lat, l_ctx):
    i = pl.program_id(2)
    kvh = pl.program_id(1)
    dn = (((1,), (1,)), ((), ()))
    band_w = TQ + 2 * SW_WINDOW
    kc = k_ref[l_lat:l_lat + l_ctx, :]
    vc = v_ref[l_lat:l_lat + l_ctx, :]

    def ctx_scores(qg):
        return lax.dot_general(qg, kc, dn, preferred_element_type=F32)

    @pl.when(i < n_lat_q)
    def _():
        q0 = i * TQ
        kstart = pl.multiple_of(jnp.clip(q0 - SW_WINDOW, 0, l_lat - band_w), SW_WINDOW)
        kb = k_ref[pl.ds(kstart, band_w), :]
        vb = v_ref[pl.ds(kstart, band_w), :]
        qpos = q0 + lax.broadcasted_iota(jnp.int32, (TQ, band_w), 0)
        kpos = kstart + lax.broadcasted_iota(jnp.int32, (TQ, band_w), 1)
        in_band = jnp.abs(qpos - kpos) <= SW_WINDOW
        for g in range(SW_GROUP):
            qg = q_ref[:, g * SW_HD:(g + 1) * SW_HD]
            sink = sink_ref[kvh * SW_GROUP + g] * LOG2_E
            sb = jnp.where(in_band, lax.dot_general(qg, kb, dn, preferred_element_type=F32), NEG_INF)
            sc = ctx_scores(qg)
            m = jnp.maximum(jnp.maximum(jnp.max(sb, axis=-1, keepdims=True), jnp.max(sc, axis=-1, keepdims=True)), sink)
            eb = jnp.exp2(sb - m)
            ec = jnp.exp2(sc - m)
            denom = jnp.sum(eb, axis=-1, keepdims=True) + jnp.sum(ec, axis=-1, keepdims=True) + jnp.exp2(sink - m)
            o = (jnp.dot(eb.astype(BF16), vb, preferred_element_type=F32)
                 + jnp.dot(ec.astype(BF16), vc, preferred_element_type=F32))
            o_ref[:, g * SW_HD:(g + 1) * SW_HD] = (o * (1.0 / denom)).astype(BF16)

    @pl.when(i == n_lat_q)
    def _():
        for g in range(SW_GROUP):
            qg = q_ref[:, g * SW_HD:(g + 1) * SW_HD]
            sink = sink_ref[kvh * SW_GROUP + g] * LOG2_E
            sc = ctx_scores(qg)
            m = jnp.maximum(jnp.max(sc, axis=-1, keepdims=True), sink)
            ec = jnp.exp2(sc - m)
            denom = jnp.sum(ec, axis=-1, keepdims=True) + jnp.exp2(sink - m)
            o = jnp.dot(ec.astype(BF16), vc, preferred_element_type=F32)
            o_ref[:, g * SW_HD:(g + 1) * SW_HD] = (o * (1.0 / denom)).astype(BF16)

    @pl.when(i > n_lat_q)
    def _():
        o_ref[...] = jnp.zeros_like(o_ref)


def _window_attention(p, sink, l_lat, l_ctx):
    b, s_pad, _ = p.shape
    gw = SW_GROUP * SW_HD
    q_blk0 = 3 * DA_WIDTH // gw
    k_blk0 = (3 * DA_WIDTH + SW_WIDTH) // SW_HD
    v_blk0 = k_blk0 + SW_KV_HEADS
    kern = functools.partial(_sw_kernel, n_lat_q=l_lat // TQ, l_lat=l_lat, l_ctx=l_ctx)
    return pl.pallas_call(
        kern,
        grid=(b, SW_KV_HEADS, s_pad // TQ),
        in_specs=[
            pl.BlockSpec(memory_space=pltpu.SMEM),
            pl.BlockSpec((None, TQ, gw), lambda bi, h, i: (bi, i, q_blk0 + h)),
            pl.BlockSpec((None, s_pad, SW_HD), lambda bi, h, i: (bi, 0, k_blk0 + h)),
            pl.BlockSpec((None, s_pad, SW_HD), lambda bi, h, i: (bi, 0, v_blk0 + h)),
        ],
        out_specs=pl.BlockSpec((None, TQ, gw), lambda bi, h, i: (bi, i, h)),
        out_shape=jax.ShapeDtypeStruct((b, s_pad, SW_WIDTH), BF16),
        compiler_params=_cparams(("parallel", "parallel", "arbitrary")),
        name="window_attention",
    )(sink, p, p, p)


def _s5_discretise(lam_re, lam_im, log_dt, b_re, b_im):
    lr = jnp.minimum(lam_re.astype(F32), SSM_RE_MAX)
    li = lam_im.astype(F32)
    dt = jnp.exp(log_dt.astype(F32))[..., None]
    mag = jnp.exp(lr * dt)
    lbr = mag * jnp.cos(li * dt)
    lbi = mag * jnp.sin(li * dt)
    den = lr * lr + li * li
    cr = ((lbr - 1.0) * lr + lbi * li) / den
    ci = (lbi * lr - (lbr - 1.0) * li) / den
    br = b_re.astype(F32)
    bi = b_im.astype(F32)
    bbr = cr[..., None] * br - ci[..., None] * bi
    bbi = cr[..., None] * bi + ci[..., None] * br
    return lbr, lbi, bbr, bbi


def _s5_matrices(lam_re, lam_im, log_dt, b_re, b_im, c_re, c_im):
    lbr, lbi, bbr, bbi = _s5_discretise(lam_re, lam_im, log_dt, b_re, b_im)
    eye = jnp.eye(SSM_GROUPS, dtype=F32)

    def in_map(bb):
        return jnp.einsum("dgnp,gh->dgphn", bb, eye).reshape(2, SSM_WIDTH, SSM_LANES)

    def out_map(cc):
        return jnp.einsum("dgpn,gh->dgnhp", cc, eye).reshape(2, SSM_LANES, SSM_WIDTH)

    b_blk = jnp.concatenate([in_map(bbr), in_map(bbi)], axis=2).astype(BF16)
    c_blk = jnp.concatenate([out_map(c_re.astype(F32)), out_map(-c_im.astype(F32))], axis=1).astype(BF16)
    lam_b = jnp.concatenate([lbr.reshape(2, 1, SSM_LANES), lbi.reshape(2, 1, SSM_LANES)], axis=2)
    lam_b = jnp.broadcast_to(lam_b, (2, 8, 2 * SSM_LANES))
    return b_blk, c_blk, lam_b


def _s5_kernel(u_ref, b_ref, c_ref, lam_ref, y_ref, bu_ref, hs_ref, h_ref, *, n_batch):
    d = pl.program_id(0)
    s = pl.program_id(1)
    rows = TT * n_batch

    @pl.when(s == 0)
    def _():
        h_ref[...] = jnp.zeros_like(h_ref)

    u = u_ref[...].astype(BF16)
    for c in range(2 * SSM_LANES // 256):
        k0 = ((c % (SSM_LANES // 256)) * 256 // SSM_STATE * SSM_GROUP_CH) // LANE * LANE
        bu_ref[:, c * 256:(c + 1) * 256] = jnp.dot(
            u[:, k0:k0 + LANE], b_ref[k0:k0 + LANE, c * 256:(c + 1) * 256], preferred_element_type=F32)

    cw = 512
    for c in range(SSM_LANES // cw):
        re = slice(c * cw, (c + 1) * cw)
        im = slice(SSM_LANES + c * cw, SSM_LANES + (c + 1) * cw)
        lr = lam_ref[:, re]
        li = lam_ref[:, im]

        def step(t, carry, re=re, im=im, lr=lr, li=li):
            hr, hi = carry
            tt = jnp.where(d == 0, t, TT - 1 - t)
            r = pl.multiple_of(tt * n_batch, n_batch)
            nr = lr * hr - li * hi + bu_ref[pl.ds(r, n_batch), re]
            ni = lr * hi + li * hr + bu_ref[pl.ds(r, n_batch), im]
            hs_ref[pl.ds(r, n_batch), re] = nr
            hs_ref[pl.ds(r, n_batch), im] = ni
            return nr, ni

        hr, hi = lax.fori_loop(0, TT, step, (h_ref[:, re], h_ref[:, im]), unroll=4)
        h_ref[:, re] = hr
        h_ref[:, im] = hi

    half = SSM_LANES // 2
    for j in range(SSM_WIDTH // 256):
        acc = jnp.dot(hs_ref[:, j * half:(j + 1) * half].astype(BF16),
                      c_ref[j * half:(j + 1) * half, j * 256:(j + 1) * 256], preferred_element_type=F32)
        acc += jnp.dot(hs_ref[:, SSM_LANES + j * half:SSM_LANES + (j + 1) * half].astype(BF16),
                       c_ref[SSM_LANES + j * half:SSM_LANES + (j + 1) * half, j * 256:(j + 1) * 256],
                       preferred_element_type=F32)
        y_ref[:, j * 256:(j + 1) * 256] = acc


def _s5_scan(u_rows, b_blk, c_blk, lam_b, n_batch, l_lat, l_ctx):
    rows = TT * n_batch
    n_lat, n_ctx = l_lat // TT, l_ctx // TT
    n_steps = n_lat + n_ctx

    def tile(d, s):
        fwd = jnp.where(s < n_ctx, n_lat + s, s - n_ctx)
        bwd = jnp.where(s < n_ctx, n_lat + n_ctx - 1 - s, n_lat - 1 - (s - n_ctx))
        return jnp.where(d == 0, fwd, bwd)

    kern = functools.partial(_s5_kernel, n_batch=n_batch)
    return pl.pallas_call(
        kern,
        grid=(2, n_steps),
        in_specs=[
            pl.BlockSpec((rows, SSM_WIDTH), lambda d, s: (tile(d, s), 0)),
            pl.BlockSpec((None, SSM_WIDTH, 2 * SSM_LANES), lambda d, s: (d, 0, 0)),
            pl.BlockSpec((None, 2 * SSM_LANES, SSM_WIDTH), lambda d, s: (d, 0, 0)),
            pl.BlockSpec((None, 8, 2 * SSM_LANES), lambda d, s: (d, 0, 0)),
        ],
        out_specs=pl.BlockSpec((None, rows, SSM_WIDTH), lambda d, s: (d, tile(d, s), 0)),
        out_shape=jax.ShapeDtypeStruct((2, (l_lat + l_ctx) * n_batch, SSM_WIDTH), F32),
        scratch_shapes=[
            pltpu.VMEM((rows, 2 * SSM_LANES), F32),
            pltpu.VMEM((rows, 2 * SSM_LANES), F32),
            pltpu.VMEM((n_batch, 2 * SSM_LANES), F32),
        ],
        compiler_params=_cparams(("arbitrary", "arbitrary")),
        name="s5_scan",
    )(u_rows, b_blk, c_blk, lam_b)


def _glu_kernel(u_ref, y_ref, d_ref, w_ref, o_ref, *, n_valid_tiles):
    @pl.when(pl.program_id(0) < n_valid_tiles)
    def _():
        y = u_ref[...] * d_ref[...] + y_ref[0] + y_ref[1]
        g = jax.nn.gelu(y)
        z = jnp.dot(g.astype(BF16), w_ref[...], preferred_element_type=F32)
        o_ref[...] = (g * jax.nn.sigmoid(z)).astype(BF16)

    @pl.when(pl.program_id(0) >= n_valid_tiles)
    def _():
        o_ref[...] = jnp.zeros_like(o_ref)


def _s5_glu(u_rows, y, d_skip, w_glu, n_valid_rows):
    n_rows = u_rows.shape[0]
    tr = 512
    n_valid_tiles = n_valid_rows // tr
    return pl.pallas_call(
        functools.partial(_glu_kernel, n_valid_tiles=n_valid_tiles),
        grid=(n_rows // tr,),
        in_specs=[
            pl.BlockSpec((tr, SSM_WIDTH), lambda i: (i, 0)),
            pl.BlockSpec((2, tr, SSM_WIDTH), lambda i: (0, jnp.minimum(i, n_valid_tiles - 1), 0)),
            pl.BlockSpec((1, SSM_WIDTH), lambda i: (0, 0)),
            pl.BlockSpec((SSM_WIDTH, SSM_WIDTH), lambda i: (0, 0)),
        ],
        out_specs=pl.BlockSpec((tr, SSM_WIDTH), lambda i: (i, 0)),
        out_shape=jax.ShapeDtypeStruct((n_rows, SSM_WIDTH), BF16),
        compiler_params=_cparams(("parallel",)),
        name="s5_glu",
    )(u_rows, y, d_skip, w_glu)


def _pack_bf16_pairs(h):
    n = h.shape[1] // 2
    bits = pltpu.bitcast(h.astype(BF16).astype(F32), jnp.uint32)
    return pltpu.bitcast(bits[:, :n] | (bits[:, n:] >> 16), F32)


def _unpack_bf16_pairs(words):
    bits = pltpu.bitcast(words, jnp.uint32)
    hi = pltpu.bitcast(bits & jnp.uint32(0xFFFF0000), F32).astype(BF16)
    lo = pltpu.bitcast(bits << 16, F32).astype(BF16)
    return hi, lo


def _kout_kernel(da_ref, sw_ref, ss_ref, x_ref, gate_ref, sh_ref, sc_ref, g_ref, w_ref, wr_ref, xh_ref, lg_ref):
    d = g_ref.shape[-1]
    hr = TM // ROW_SPLIT
    for r in range(ROW_SPLIT):
        rows = slice(r * hr, (r + 1) * hr)
        acc = jnp.dot(da_ref[rows, :], w_ref[0:DA_WIDTH, :], preferred_element_type=F32)
        acc += jnp.dot(sw_ref[rows, :], w_ref[DA_WIDTH:DA_WIDTH + SW_WIDTH, :], preferred_element_type=F32)
        acc += jnp.dot(ss_ref[rows, :], w_ref[DA_WIDTH + SW_WIDTH:, :], preferred_element_type=F32)
        x1 = x_ref[rows, :] + gate_ref[...] * acc
        xh_ref[rows, 0:d] = x1
        y = x1 * lax.rsqrt(jnp.mean(x1 * x1, axis=-1, keepdims=True) + NORM_EPS) * g_ref[...]
        h2 = y * (1.0 + sc_ref[...]) + sh_ref[...]
        xh_ref[rows, d:d + d // 2] = _pack_bf16_pairs(h2)
        lg_ref[rows, :] = jnp.dot(h2.astype(BF16), wr_ref[...], preferred_element_type=F32)


def _kout(da, sw, ss, xall, mods4, g2, w_out, layer, w_router, n_lat_tiles):
    b, s_pad = da.shape[0], da.shape[1]
    d = w_out.shape[2]

    def mrow(k):
        return lambda bi, i: (jnp.where(i >= n_lat_tiles, 8, bi), k, 0, 0)

    tok = lambda bi, i: (bi, i, 0)
    return pl.pallas_call(
        _kout_kernel,
        grid=(b, s_pad // TM),
        in_specs=[
            pl.BlockSpec((None, TM, DA_WIDTH), tok),
            pl.BlockSpec((None, TM, SW_WIDTH), tok),
            pl.BlockSpec((TM, SSM_WIDTH), lambda bi, i: (i, bi)),
            pl.BlockSpec((None, TM, d), tok),
            pl.BlockSpec((None, None, 1, d), mrow(2)),
            pl.BlockSpec((None, None, 1, d), mrow(3)),
            pl.BlockSpec((None, None, 1, d), mrow(4)),
            pl.BlockSpec((1, d), lambda bi, i: (0, 0)),
            pl.BlockSpec((None, d, d), lambda bi, i: (layer, 0, 0), pipeline_mode=pl.Buffered(1)),
            pl.BlockSpec((d, LANE), lambda bi, i: (0, 0)),
        ],
        out_specs=[
            pl.BlockSpec((None, TM, d + d // 2), tok),
            pl.BlockSpec((None, TM, LANE), tok),
        ],
        out_shape=[
            jax.ShapeDtypeStruct((b, s_pad, d + d // 2), F32),
            jax.ShapeDtypeStruct((b, s_pad, LANE), F32),
        ],
        compiler_params=_cparams(("parallel", "parallel")),
        name="outproj_norm_router",
    )(da, sw, ss, xall, mods4, mods4, mods4, g2, w_out, w_router)


def _lane_cumsum(x, tri):
    n = x.shape[1]
    total = jnp.zeros((x.shape[0], 1), F32)
    parts = []
    for c in range(n // 256):
        loc = jnp.dot(x[:, c * 256:(c + 1) * 256].astype(BF16), tri, preferred_element_type=F32)
        parts.append(loc + total)
        total = total + loc[:, 255:256]
    return jnp.concatenate(parts, axis=1)


def _route_kernel(lg_ref, tok_ref, idx_ref, gate_ref, *, n_tok, cap):
    lg = lg_ref[...]
    lane = lax.broadcasted_iota(jnp.int32, lg.shape, 1)
    lg = jnp.where(lane < N_EXPERTS, lg, NEG_INF)
    e = jnp.exp(lg - jnp.max(lg, axis=-1, keepdims=True))
    aff = e / jnp.sum(e, axis=-1, keepdims=True)
    aff_t = jnp.transpose(aff)[0:N_EXPERTS, :]
    bits = pltpu.bitcast(aff_t, jnp.int32)

    def search(k, thr):
        cand = thr | lax.shift_left(jnp.int32(1), 30 - k)
        cnt = jnp.sum((bits >= cand).astype(F32), axis=-1, keepdims=True)
        return jnp.where(cnt >= cap, cand, thr)

    thr = lax.fori_loop(0, 31, search, jnp.zeros((N_EXPERTS, 1), jnp.int32))
    gt = bits > thr
    eq = bits == thr
    r_i = lax.broadcasted_iota(jnp.int32, (256, 256), 0)
    c_i = lax.broadcasted_iota(jnp.int32, (256, 256), 1)
    tri = (r_i <= c_i).astype(BF16)
    need = cap - jnp.sum(gt.astype(F32), axis=-1, keepdims=True)
    eq_f = eq.astype(F32)
    eq_rank = _lane_cumsum(eq_f, tri) - eq_f
    sel = jnp.where(gt, 1.0, jnp.where(eq & (eq_rank < need), 1.0, 0.0))
    pos = _lane_cumsum(sel, tri) - sel
    pos = jnp.where(sel > 0.0, pos, -1.0)

    a_hi = aff.astype(BF16)
    r1 = aff - a_hi.astype(F32)
    a_mid = r1.astype(BF16)
    a_lo = (r1 - a_mid.astype(F32)).astype(BF16)
    tok = tok_ref[...]
    payload = jnp.where(lane < 16, a_hi.astype(F32),
                        jnp.where(lane < 32, pltpu.roll(a_mid.astype(F32), 16, 1),
                                  jnp.where(lane < 48, pltpu.roll(a_lo.astype(F32), 32, 1), tok))).astype(BF16)

    slot = lax.broadcasted_iota(jnp.int32, (cap, 256), 0).astype(F32)
    out_lane = lax.broadcasted_iota(jnp.int32, (cap, LANE), 1)
    for ex in range(N_EXPERTS):
        acc = jnp.zeros((cap, LANE), F32)
        for c in range(n_tok // 256):
            onehot = jnp.where(pos[ex:ex + 1, c * 256:(c + 1) * 256] == slot, 1.0, 0.0).astype(BF16)
            acc += jnp.dot(onehot, payload[c * 256:(c + 1) * 256, :], preferred_element_type=F32)
        pick = lambda l: jnp.sum(jnp.where(out_lane == l, acc, 0.0), axis=-1, keepdims=True)
        gate_ref[ex] = (pick(ex) + pick(16 + ex)) + pick(32 + ex)
        idx_ref[ex] = (pick(48) * 64.0 + pick(49)).astype(jnp.int32)


def _route(lg, tok_cols, row_block, n_tok):
    b = lg.shape[0]
    cap = EC_CAPACITY * n_tok // N_EXPERTS
    kern = functools.partial(_route_kernel, n_tok=n_tok, cap=cap)
    idx, gate = pl.pallas_call(
        kern,
        grid=(b,),
        in_specs=[
            pl.BlockSpec((None, n_tok, LANE), lambda bi: (bi, row_block, 0)),
            pl.BlockSpec((n_tok, LANE), lambda bi: (0, 0)),
        ],
        out_specs=[
            pl.BlockSpec((None, N_EXPERTS, cap, 1), lambda bi: (bi, 0, 0, 0)),
            pl.BlockSpec((None, N_EXPERTS, cap, 1), lambda bi: (bi, 0, 0, 0)),
        ],
        out_shape=[
            jax.ShapeDtypeStruct((b, N_EXPERTS, cap, 1), jnp.int32),
            jax.ShapeDtypeStruct((b, N_EXPERTS, cap, 1), F32),
        ],
        compiler_params=_cparams(("parallel",)),
        name="route",
    )(lg, tok_cols)
    return idx[..., 0], gate


def _token_columns(n_tok):
    t = jnp.arange(n_tok, dtype=jnp.int32)[:, None]
    lane = jnp.arange(LANE, dtype=jnp.int32)[None, :]
    return jnp.where(lane == 48, t // 64, jnp.where(lane == 49, t % 64, 0)).astype(F32)


ROW_SLOTS = 3
DMA_UNROLL = 8


def _ffn_kernel(idx_prev_ref, idx_ref, idx_next_ref, gate_ref, m5_ref, m5c_ref, wg_ref, wu_ref, wd_ref, xh_in_hbm,
                xh_hbm, rbuf, sem_in, sem_out, *, n_rows, n_lat_rows, n_batch):
    del xh_in_hbm
    d = wd_ref.shape[-1]
    ff = wg_ref.shape[-1]
    b = pl.program_id(1)
    s = pl.program_id(0) * n_batch + b
    n_steps = N_EXPERTS * n_batch
    slot = s % ROW_SLOTS
    next_slot = (s + 1) % ROW_SLOTS
    prev_slot = (s + 2) % ROW_SLOTS
    next_sample = (b + 1) % n_batch
    prev_sample = (b + n_batch - 1) % n_batch

    def gather_row(idx_r, sample, to_slot, c):
        return pltpu.make_async_copy(xh_hbm.at[sample, pl.ds(idx_r[0, c], 1), :], rbuf.at[to_slot, pl.ds(c, 1), :],
                                     sem_in.at[to_slot])

    def scatter_row(sample, t, from_slot, c):
        return pltpu.make_async_copy(rbuf.at[from_slot, pl.ds(c, 1), pl.ds(0, d)],
                                     xh_hbm.at[sample, pl.ds(t, 1), pl.ds(0, d)], sem_out.at[from_slot])

    def wait_rows_in(at_slot):
        pltpu.make_async_copy(xh_hbm.at[0, pl.ds(0, n_rows), :], rbuf.at[at_slot], sem_in.at[at_slot]).wait()

    def wait_rows_out(at_slot):
        pltpu.make_async_copy(rbuf.at[at_slot, :, pl.ds(0, d)], xh_hbm.at[0, pl.ds(0, n_rows), pl.ds(0, d)],
                              sem_out.at[at_slot]).wait()

    @pl.when(s == 0)
    def _():
        def body(c, carry):
            gather_row(idx_ref, b, slot, c).start()
            gather_row(idx_prev_ref, prev_sample, prev_slot, c).start()
            return carry

        lax.fori_loop(0, n_rows, body, 0, unroll=DMA_UNROLL)
        wait_rows_in(prev_slot)

    @pl.when(s >= 1)
    def _():
        wait_rows_out(next_slot)

    wait_rows_in(slot)

    fc = 256
    n_units = ff // fc + d // fc
    per_unit = -(-n_rows // n_units)
    issued = [0]

    def issue_row_copies():
        for c in range(issued[0], min(issued[0] + per_unit, n_rows)):
            gather_row(idx_next_ref, next_sample, next_slot, c).start()
            scatter_row(prev_sample, idx_prev_ref[0, c], prev_slot, c).start()
        issued[0] = min(issued[0] + per_unit, n_rows)

    x_hi, x_lo = _unpack_bf16_pairs(rbuf[slot, :, d:d + d // 2])
    hid = []
    for j in range(ff // fc):
        cols = slice(j * fc, (j + 1) * fc)
        hg = (jnp.dot(x_hi, wg_ref[0:d // 2, cols], preferred_element_type=F32)
              + jnp.dot(x_lo, wg_ref[d // 2:d, cols], preferred_element_type=F32))
        hu = (jnp.dot(x_hi, wu_ref[0:d // 2, cols], preferred_element_type=F32)
              + jnp.dot(x_lo, wu_ref[d // 2:d, cols], preferred_element_type=F32))
        hid.append((hg * jax.nn.sigmoid(hg) * hu).astype(BF16))
        issue_row_copies()
    hid = jnp.concatenate(hid, axis=1)
    row = lax.broadcasted_iota(jnp.int32, (n_rows, 1), 0)
    is_lat = row < n_lat_rows
    gate = gate_ref[...]
    for j in range(d // fc):
        cols = slice(j * fc, (j + 1) * fc)
        y = jnp.dot(hid, wd_ref[:, cols], preferred_element_type=F32)
        m5 = jnp.where(is_lat, m5_ref[:, cols], m5c_ref[:, cols])
        rbuf[slot, :, cols] = rbuf[slot, :, cols] + m5 * (y * gate)
        issue_row_copies()
    assert issued[0] == n_rows

    @pl.when(s == n_steps - 1)
    def _():
        wait_rows_in(next_slot)
        wait_rows_out(prev_slot)

        def body(c, carry):
            scatter_row(b, idx_ref[0, c], slot, c).start()
            return carry

        lax.fori_loop(0, n_rows, body, 0, unroll=DMA_UNROLL)
        wait_rows_out(slot)


def _expert_ffn(idx, gate, mods4, w_gate, w_up, w_down, layer, xh, n_lat_rows):
    b, n_xh_rows, row_w = xh.shape
    n_rows = idx.shape[2]
    _, _, d, ff = w_gate.shape
    kern = functools.partial(_ffn_kernel, n_rows=n_rows, n_lat_rows=n_lat_rows, n_batch=b)
    idx4 = idx[:, :, None, :]
    steps = jnp.swapaxes(idx, 0, 1).reshape(N_EXPERTS * b, n_rows)
    idx_prev4 = jnp.concatenate([steps[b - 1:b], steps[:-1]], axis=0).reshape(N_EXPERTS, b, 1, n_rows)

    def next_step(e, bi):
        return (bi + 1) % b, jnp.minimum(e + (bi + 1) // b, N_EXPERTS - 1), 0, 0

    return pl.pallas_call(
        kern,
        grid=(N_EXPERTS, b),
        in_specs=[
            pl.BlockSpec((None, None, 1, n_rows), lambda e, bi: (e, bi, 0, 0), memory_space=pltpu.SMEM),
            pl.BlockSpec((None, None, 1, n_rows), lambda e, bi: (bi, e, 0, 0), memory_space=pltpu.SMEM),
            pl.BlockSpec((None, None, 1, n_rows), next_step, memory_space=pltpu.SMEM),
            pl.BlockSpec((None, None, n_rows, 1), lambda e, bi: (bi, e, 0, 0)),
            pl.BlockSpec((None, None, 1, d), lambda e, bi: (bi, 5, 0, 0)),
            pl.BlockSpec((None, None, 1, d), lambda e, bi: (8, 5, 0, 0)),
            pl.BlockSpec((None, None, d, ff), lambda e, bi: (layer, e, 0, 0)),
            pl.BlockSpec((None, None, d, ff), lambda e, bi: (layer, e, 0, 0)),
            pl.BlockSpec((None, None, ff, d), lambda e, bi: (layer, e, 0, 0)),
            pl.BlockSpec(memory_space=pl.ANY),
        ],
        out_specs=pl.BlockSpec(memory_space=pl.ANY),
        out_shape=jax.ShapeDtypeStruct((b, n_xh_rows, row_w), F32),
        scratch_shapes=[
            pltpu.VMEM((ROW_SLOTS, n_rows, row_w), F32),
            pltpu.SemaphoreType.DMA((ROW_SLOTS,)),
            pltpu.SemaphoreType.DMA((ROW_SLOTS,)),
        ],
        input_output_aliases={9: 0},
        compiler_params=pltpu.CompilerParams(
            dimension_semantics=("arbitrary", "arbitrary"), vmem_limit_bytes=VMEM_LIMIT, has_side_effects=True),
        name="expert_ffn",
    )(idx_prev4, idx4, idx4, gate, mods4, mods4, w_gate, w_up, w_down, xh)


def _final_kernel(x_ref, g_ref, o_ref):
    x = x_ref[...]
    o_ref[...] = x * lax.rsqrt(jnp.mean(x * x, axis=-1, keepdims=True) + NORM_EPS) * g_ref[...]


def _final_norm(xall, g, l_lat):
    b, d = xall.shape[0], g.shape[-1]
    return pl.pallas_call(
        _final_kernel,
        grid=(b, l_lat // TM),
        in_specs=[pl.BlockSpec((None, TM, d), lambda bi, i: (bi, i, 0)), pl.BlockSpec((1, d), lambda bi, i: (0, 0))],
        out_specs=pl.BlockSpec((None, TM, d), lambda bi, i: (bi, i, 0)),
        out_shape=jax.ShapeDtypeStruct((b, l_lat, d), F32),
        compiler_params=_cparams(("parallel", "parallel")),
        name="final_norm",
    )(xall, g)


def _rope_table(l_lat, s_pad, head_dim):
    q = head_dim // 4
    inv_freq = ROPE_THETA ** (-jnp.arange(q, dtype=F32) / q)
    pos = jnp.arange(l_lat)
    ang_r = (pos // GRID_W).astype(F32)[:, None] * inv_freq[None, :]
    ang_c = (pos % GRID_W).astype(F32)[:, None] * inv_freq[None, :]
    ang = jnp.concatenate([ang_r, ang_r, ang_c, ang_c], axis=-1)
    ang = jnp.tile(ang, (1, LANE // head_dim))
    lane = jnp.arange(LANE)[None, :]
    first = (lane % (2 * q)) < q
    cos, sin = jnp.cos(ang), jnp.sin(ang)
    tab = jnp.concatenate([cos, jnp.where(first, -sin, 0.0), jnp.where(first, 0.0, sin)], axis=-1)
    ident = jnp.concatenate([jnp.ones((s_pad - l_lat, LANE), F32), jnp.zeros((s_pad - l_lat, 2 * LANE), F32)], axis=-1)
    return jnp.concatenate([tab, ident], axis=0)


def kernel(x, c, ctx, c_ctx, w_mod, b_mod, norm1_g, norm2_g, w_in, w_out, da_lam_q1, da_lam_k1, da_lam_q2, da_lam_k2, da_subln_g, sw_sink, ssm_lam_re, ssm_lam_im, ssm_log_dt, ssm_b_re, ssm_b_im, ssm_c_re, ssm_c_im, ssm_d, ssm_w_glu, w_router, w_gate, w_up, w_down, final_g):
    b, l_lat, d = x.shape
    l_ctx = ctx.shape[1]
    depth = w_mod.shape[0]
    ctx_pad = -(-l_ctx // TM) * TM
    s_pad = l_lat + ctx_pad
    n_lat_tiles = l_lat // TM
    assert b == 8 and l_lat % TM == 0 and l_ctx % TQ == 0 and l_lat % GRID_W == 0

    xall = jnp.concatenate([x, ctx, jnp.zeros((b, ctx_pad - l_ctx, d), F32)], axis=1)
    craw = jnp.concatenate([c, c_ctx[None, :], jnp.zeros((16 - b - 1, d), F32)], axis=0)
    mods = _mods(craw, w_mod, b_mod)
    rope_da = _rope_table(l_lat, s_pad, DA_QK)
    rope_sw = _rope_table(l_lat, s_pad, SW_HD)
    rope = jnp.concatenate([rope_da * DA_Q_SCALE, rope_da, rope_sw * SW_Q_SCALE, rope_sw], axis=-1)
    tok_lat = _token_columns(l_lat)
    tok_ctx = _token_columns(l_ctx)
    w_in_b, w_out_b = w_in.astype(BF16), w_out.astype(BF16)
    w_gate_b, w_up_b, w_down_b = w_gate.astype(BF16), w_up.astype(BF16), w_down.astype(BF16)

    for l in range(depth):
        last = l == depth - 1
        lambda_init = 0.8 - 0.6 * math.exp(-0.3 * l)
        mods4 = mods[l].reshape(16, N_MOD, 1, d)
        p, u = _k1(xall, mods4, norm1_g[l][None, :], w_in_b, l, rope, n_lat_tiles)

        lam_params = jnp.zeros((8, LANE), F32).at[0:4, 0:DA_QK].set(
            jnp.stack([da_lam_q1[l], da_lam_k1[l], da_lam_q2[l], da_lam_k2[l]]).astype(F32))
        da = _diff_attention(p, lam_params, da_subln_g[l][None, :].astype(F32), l_lat, l_ctx, lambda_init)
        sw = _window_attention(p, sw_sink[l].astype(F32), l_lat, l_ctx)

        b_blk, c_blk, lam_b = _s5_matrices(ssm_lam_re[l], ssm_lam_im[l], ssm_log_dt[l], ssm_b_re[l], ssm_b_im[l],
                                           ssm_c_re[l], ssm_c_im[l])
        u_rows = u.reshape(s_pad * b, SSM_WIDTH)
        y = _s5_scan(u_rows, b_blk, c_blk, lam_b, b, l_lat, l_ctx)
        ss = _s5_glu(u_rows, y, ssm_d[l][None, :].astype(F32), ssm_w_glu[l].astype(BF16), (l_lat + l_ctx) * b)
        ss = ss.reshape(s_pad, b * SSM_WIDTH)

        w_r = jnp.zeros((d, LANE), BF16).at[:, 0:N_EXPERTS].set(w_router[l].astype(BF16))
        xh, lg = _kout(da, sw, ss, xall, mods4, norm2_g[l][None, :], w_out_b, l, w_r, n_lat_tiles)

        idx, gate = _route(lg, tok_lat, 0, l_lat)
        if not last:
            idx_c, gate_c = _route(lg, tok_ctx, l_lat // l_ctx, l_ctx)
            idx = jnp.concatenate([idx, idx_c + l_lat], axis=2)
            gate = jnp.concatenate([gate, gate_c], axis=2)
        xall = _expert_ffn(idx, gate, mods4, w_gate_b, w_up_b, w_down_b, l, xh, EC_CAPACITY * l_lat // N_EXPERTS)

    return _final_norm(xall, final_g[None, :], l_lat)
```

```python
import functools
import math

import jax
import jax.numpy as jnp
from jax import lax
from jax.experimental import pallas as pl
from jax.experimental.pallas import tpu as pltpu

F32 = jnp.float32
BF16 = jnp.bfloat16

GRID_W = 64
ROPE_THETA = 10000.0
NORM_EPS = 1e-6
NEG_INF = -1e30

DA_HEADS = 6
DA_QK = 64
DA_V = 128
DA_WIDTH = 768
SW_HEADS = 6
SW_KV_HEADS = 2
SW_GROUP = SW_HEADS // SW_KV_HEADS
SW_HD = 128
SW_WINDOW = 128
SW_WIDTH = 768
SSM_GROUPS = 32
SSM_GROUP_CH = 16
SSM_STATE = 64
SSM_WIDTH = 512
SSM_LANES = SSM_GROUPS * SSM_STATE
SSM_RE_MAX = -1e-4
ATT_COLS = 3 * DA_WIDTH + SW_WIDTH + 2 * SW_KV_HEADS * SW_HD
N_EXPERTS = 16
EC_CAPACITY = 2
N_MOD = 6

LOG2_E = math.log2(math.e)
DA_Q_SCALE = DA_QK ** -0.5 * LOG2_E
SW_Q_SCALE = SW_HD ** -0.5 * LOG2_E

TM = 512
TQ = 512
DA_TQ = 512
TT = 64
LANE = 128
VMEM_LIMIT = 56 * 1024 * 1024


def _cparams(sem):
    return pltpu.CompilerParams(dimension_semantics=sem, vmem_limit_bytes=VMEM_LIMIT)


def _mods_kernel(c_ref, w_ref, b_ref, o_ref):
    c = c_ref[...]
    s = (c * jax.nn.sigmoid(c)).astype(BF16)
    o_ref[...] = jnp.dot(s, w_ref[...].astype(BF16), preferred_element_type=F32) + b_ref[...]


def _mods(craw, w_mod, b_mod):
    depth, d, n = w_mod.shape
    tn = 1024
    return pl.pallas_call(
        _mods_kernel,
        grid=(depth, n // tn),
        in_specs=[
            pl.BlockSpec((16, d), lambda l, j: (0, 0)),
            pl.BlockSpec((None, d, tn), lambda l, j: (l, 0, j)),
            pl.BlockSpec((None, 1, tn), lambda l, j: (l, 0, j)),
        ],
        out_specs=pl.BlockSpec((None, 16, tn), lambda l, j: (l, 0, j)),
        out_shape=jax.ShapeDtypeStruct((depth, 16, n), F32),
        compiler_params=_cparams(("parallel", "parallel")),
        name="mods",
    )(craw, w_mod, b_mod.reshape(depth, 1, n))


def _rope(acc, cos, sa, sb, quarter):
    return (acc * cos + pltpu.roll(acc, LANE - quarter, 1) * sa + pltpu.roll(acc, quarter, 1) * sb)


ROW_SPLIT = 2


def _k1_kernel(x_ref, sh_ref, sc_ref, g_ref, w_ref, rope_ref, p_ref, u_ref):
    n_chunks = (ATT_COLS + SSM_WIDTH) // 256
    hr = TM // ROW_SPLIT
    for r in range(ROW_SPLIT):
        rows = slice(r * hr, (r + 1) * hr)
        x = x_ref[rows, :]
        y = x * lax.rsqrt(jnp.mean(x * x, axis=-1, keepdims=True) + NORM_EPS) * g_ref[...]
        h = (y * (1.0 + sc_ref[...]) + sh_ref[...]).astype(BF16)
        for j in range(n_chunks):
            c0 = j * 256
            acc = jnp.dot(h, w_ref[:, c0:c0 + 256], preferred_element_type=F32)
            if c0 >= ATT_COLS:
                u_ref[rows, c0 - ATT_COLS:c0 - ATT_COLS + 256] = acc
                continue
            if c0 < DA_WIDTH:
                t0, quarter = 0, DA_QK // 4
            elif c0 < 2 * DA_WIDTH:
                t0, quarter = 3 * LANE, DA_QK // 4
            elif 3 * DA_WIDTH <= c0 < 3 * DA_WIDTH + SW_WIDTH:
                t0, quarter = 6 * LANE, SW_HD // 4
            elif 3 * DA_WIDTH + SW_WIDTH <= c0 < 3 * DA_WIDTH + SW_WIDTH + SW_KV_HEADS * SW_HD:
                t0, quarter = 9 * LANE, SW_HD // 4
            else:
                t0 = None
            if t0 is not None:
                cos = rope_ref[rows, t0:t0 + LANE]
                sa = rope_ref[rows, t0 + LANE:t0 + 2 * LANE]
                sb = rope_ref[rows, t0 + 2 * LANE:t0 + 3 * LANE]
                acc = jnp.concatenate(
                    [_rope(acc[:, :LANE], cos, sa, sb, quarter), _rope(acc[:, LANE:], cos, sa, sb, quarter)], axis=1)
            p_ref[rows, c0:c0 + 256] = acc.astype(BF16)


def _k1(xall, mods4, g1, w_in, layer, rope, n_lat_tiles):
    b, s_pad = xall.shape[0], rope.shape[0]
    _, d, ncols = w_in.shape

    def mrow(k):
        return lambda bi, i: (jnp.where(i >= n_lat_tiles, 8, bi), k, 0, 0)

    return pl.pallas_call(
        _k1_kernel,
        grid=(b, s_pad // TM),
        in_specs=[
            pl.BlockSpec((None, TM, d), lambda bi, i: (bi, i, 0)),
            pl.BlockSpec((None, None, 1, d), mrow(0)),
            pl.BlockSpec((None, None, 1, d), mrow(1)),
            pl.BlockSpec((1, d), lambda bi, i: (0, 0)),
            pl.BlockSpec((None, d, ncols), lambda bi, i: (layer, 0, 0), pipeline_mode=pl.Buffered(1)),
            pl.BlockSpec((TM, 12 * LANE), lambda bi, i: (i, 0)),
        ],
        out_specs=[
            pl.BlockSpec((None, TM, ATT_COLS), lambda bi, i: (bi, i, 0)),
            pl.BlockSpec((TM, SSM_WIDTH), lambda bi, i: (i, bi)),
        ],
        out_shape=[
            jax.ShapeDtypeStruct((b, s_pad, ATT_COLS), BF16),
            jax.ShapeDtypeStruct((s_pad, b * SSM_WIDTH), F32),
        ],
        compiler_params=_cparams(("parallel", "parallel")),
        name="norm_inproj",
    )(xall, mods4, mods4, g1, w_in, rope)


def _exp2_parts(s):
    m = jnp.max(s, axis=-1, keepdims=True)
    total = jnp.zeros_like(m)
    chunks = []
    for c in range(0, s.shape[1], 256):
        e = jnp.exp2(s[:, c:c + 256] - m)
        total = total + jnp.sum(e, axis=-1, keepdims=True)
        chunks.append(e.astype(BF16))
    return jnp.concatenate(chunks, axis=1), total


def _da_kernel(q_ref, k_ref, v_ref, lp_ref, g_ref, o_ref, *, n_lat_q, l_lat, l_ctx, lambda_init):
    i = pl.program_id(2)
    lp = lp_ref[...]
    lam = (jnp.exp(jnp.sum(lp[0:1] * lp[1:2], axis=-1, keepdims=True))
           - jnp.exp(jnp.sum(lp[2:3] * lp[3:4], axis=-1, keepdims=True)) + lambda_init)

    def attend(k, v):
        q = q_ref[...]
        lane = lax.broadcasted_iota(jnp.int32, q.shape, 1)
        zero = jnp.zeros_like(q)
        dn = (((1,), (1,)), ((), ()))
        e1, l1 = _exp2_parts(lax.dot_general(jnp.where(lane < DA_QK, q, zero), k, dn, preferred_element_type=F32))
        e2, l2 = _exp2_parts(lax.dot_general(jnp.where(lane >= DA_QK, q, zero), k, dn, preferred_element_type=F32))
        o1 = jnp.dot(e1.astype(BF16), v, preferred_element_type=F32)
        o2 = jnp.dot(e2.astype(BF16), v, preferred_element_type=F32)
        o = o1 * (1.0 / l1) - o2 * (lam / l2)
        o = o * lax.rsqrt(jnp.mean(o * o, axis=-1, keepdims=True) + NORM_EPS) * g_ref[...]
        o_ref[...] = (o * (1.0 - lambda_init)).astype(BF16)

    @pl.when(i < n_lat_q)
    def _():
        attend(k_ref[0:l_lat + l_ctx, :], v_ref[0:l_lat + l_ctx, :])

    @pl.when(i == n_lat_q)
    def _():
        attend(k_ref[l_lat:l_lat + l_ctx, :], v_ref[l_lat:l_lat + l_ctx, :])

    @pl.when(i > n_lat_q)
    def _():
        o_ref[...] = jnp.zeros_like(o_ref)


def _diff_attention(p, lam_params, subln_g, l_lat, l_ctx, lambda_init):
    b, s_pad, _ = p.shape
    kern = functools.partial(_da_kernel, n_lat_q=l_lat // DA_TQ, l_lat=l_lat, l_ctx=l_ctx, lambda_init=lambda_init)
    return pl.pallas_call(
        kern,
        grid=(b, DA_HEADS, s_pad // DA_TQ),
        in_specs=[
            pl.BlockSpec((None, DA_TQ, LANE), lambda bi, h, i: (bi, i, h)),
            pl.BlockSpec((None, s_pad, LANE), lambda bi, h, i: (bi, 0, DA_HEADS + h)),
            pl.BlockSpec((None, s_pad, LANE), lambda bi, h, i: (bi, 0, 2 * DA_HEADS + h)),
            pl.BlockSpec((8, LANE), lambda bi, h, i: (0, 0)),
            pl.BlockSpec((1, LANE), lambda bi, h, i: (0, 0)),
        ],
        out_specs=pl.BlockSpec((None, DA_TQ, LANE), lambda bi, h, i: (bi, i, h)),
        out_shape=jax.ShapeDtypeStruct((b, s_pad, DA_WIDTH), BF16),
        compiler_params=_cparams(("parallel", "parallel", "arbitrary")),
        name="diff_attention",
    )(p, p, p, lam_params, subln_g)


def _sw_kernel(sink_ref, q_ref, k_ref, v_ref, o_ref, *, n_lat_q, l_lat, l_ctx):
    i = pl.program_id(2)
    kvh = pl.program_id(1)
    dn = (((1,), (1,)), ((), ()))
    band_w = TQ + 2 * SW_WINDOW
    kc = k_ref[l_lat:l_lat + l_ctx, :]
    vc = v_ref[l_lat:l_lat + l_ctx, :]

    def ctx_scores(qg):
        return lax.dot_general(qg, kc, dn, preferred_element_type=F32)

    @pl.when(i < n_lat_q)
    def _():
        q0 = i * TQ
        kstart = pl.multiple_of(jnp.clip(q0 - SW_WINDOW, 0, l_lat - band_w), SW_WINDOW)
        kb = k_ref[pl.ds(kstart, band_w), :]
        vb = v_ref[pl.ds(kstart, band_w), :]
        qpos = q0 + lax.broadcasted_iota(jnp.int32, (TQ, band_w), 0)
        kpos = kstart + lax.broadcasted_iota(jnp.int32, (TQ, band_w), 1)
        in_band = jnp.abs(qpos - kpos) <= SW_WINDOW
        for g in range(SW_GROUP):
            qg = q_ref[:, g * SW_HD:(g + 1) * SW_HD]
            sink = sink_ref[kvh * SW_GROUP + g] * LOG2_E
            sb = jnp.where(in_band, lax.dot_general(qg, kb, dn, preferred_element_type=F32), NEG_INF)
            sc = ctx_scores(qg)
            m = jnp.maximum(jnp.maximum(jnp.max(sb, axis=-1, keepdims=True), jnp.max(sc, axis=-1, keepdims=True)), sink)
            eb = jnp.exp2(sb - m)
            ec = jnp.exp2(sc - m)
            denom = jnp.sum(eb, axis=-1, keepdims=True) + jnp.sum(ec, axis=-1, keepdims=True) + jnp.exp2(sink - m)
            o = (jnp.dot(eb.astype(BF16), vb, preferred_element_type=F32)
                 + jnp.dot(ec.astype(BF16), vc, preferred_element_type=F32))
            o_ref[:, g * SW_HD:(g + 1) * SW_HD] = (o * (1.0 / denom)).astype(BF16)

    @pl.when(i == n_lat_q)
    def _():
        for g in range(SW_GROUP):
            qg = q_ref[:, g * SW_HD:(g + 1) * SW_HD]
            sink = sink_ref[kvh * SW_GROUP + g] * LOG2_E
            sc = ctx_scores(qg)
            m = jnp.maximum(jnp.max(sc, axis=-1, keepdims=True), sink)
            ec = jnp.exp2(sc - m)
            denom = jnp.sum(ec, axis=-1, keepdims=True) + jnp.exp2(sink - m)
            o = jnp.dot(ec.astype(BF16), vc, preferred_element_type=F32)
            o_ref[:, g * SW_HD:(g + 1) * SW_HD] = (o * (1.0 / denom)).astype(BF16)

    @pl.when(i > n_lat_q)
    def _():
        o_ref[...] = jnp.zeros_like(o_ref)


def _window_attention(p, sink, l_lat, l_ctx):
    b, s_pad, _ = p.shape
    gw = SW_GROUP * SW_HD
    q_blk0 = 3 * DA_WIDTH // gw
    k_blk0 = (3 * DA_WIDTH + SW_WIDTH) // SW_HD
    v_blk0 = k_blk0 + SW_KV_HEADS
    kern = functools.partial(_sw_kernel, n_lat_q=l_lat // TQ, l_lat=l_lat, l_ctx=l_ctx)
    return pl.pallas_call(
        kern,
        grid=(b, SW_KV_HEADS, s_pad // TQ),
        in_specs=[
            pl.BlockSpec(memory_space=pltpu.SMEM),
            pl.BlockSpec((None, TQ, gw), lambda bi, h, i: (bi, i, q_blk0 + h)),
            pl.BlockSpec((None, s_pad, SW_HD), lambda bi, h, i: (bi, 0, k_blk0 + h)),
            pl.BlockSpec((None, s_pad, SW_HD), lambda bi, h, i: (bi, 0, v_blk0 + h)),
        ],
        out_specs=pl.BlockSpec((None, TQ, gw), lambda bi, h, i: (bi, i, h)),
        out_shape=jax.ShapeDtypeStruct((b, s_pad, SW_WIDTH), BF16),
        compiler_params=_cparams(("parallel", "parallel", "arbitrary")),
        name="window_attention",
    )(sink, p, p, p)


def _s5_discretise(lam_re, lam_im, log_dt, b_re, b_im):
    lr = jnp.minimum(lam_re.astype(F32), SSM_RE_MAX)
    li = lam_im.astype(F32)
    dt = jnp.exp(log_dt.astype(F32))[..., None]
    mag = jnp.exp(lr * dt)
    lbr = mag * jnp.cos(li * dt)
    lbi = mag * jnp.sin(li * dt)
    den = lr * lr + li * li
    cr = ((lbr - 1.0) * lr + lbi * li) / den
    ci = (lbi * lr - (lbr - 1.0) * li) / den
    br = b_re.astype(F32)
    bi = b_im.astype(F32)
    bbr = cr[..., None] * br - ci[..., None] * bi
    bbi = cr[..., None] * bi + ci[..., None] * br
    return lbr, lbi, bbr, bbi


def _s5_matrices(lam_re, lam_im, log_dt, b_re, b_im, c_re, c_im):
    lbr, lbi, bbr, bbi = _s5_discretise(lam_re, lam_im, log_dt, b_re, b_im)
    eye = jnp.eye(SSM_GROUPS, dtype=F32)

    def in_map(bb):
        return jnp.einsum("dgnp,gh->dgphn", bb, eye).reshape(2, SSM_WIDTH, SSM_LANES)

    def out_map(cc):
        return jnp.einsum("dgpn,gh->dgnhp", cc, eye).reshape(2, SSM_LANES, SSM_WIDTH)

    b_blk = jnp.concatenate([in_map(bbr), in_map(bbi)], axis=2).astype(BF16)
    c_blk = jnp.concatenate([out_map(c_re.astype(F32)), out_map(-c_im.astype(F32))], axis=1).astype(BF16)
    lam_b = jnp.concatenate([lbr.reshape(2, 1, SSM_LANES), lbi.reshape(2, 1, SSM_LANES)], axis=2)
    lam_b = jnp.broadcast_to(lam_b, (2, 8, 2 * SSM_LANES))
    return b_blk, c_blk, lam_b


def _s5_kernel(u_ref, b_ref, c_ref, lam_ref, y_ref, bu_ref, hs_ref, h_ref, *, n_batch):
    d = pl.program_id(0)
    s = pl.program_id(1)
    rows = TT * n_batch

    @pl.when(s == 0)
    def _():
        h_ref[...] = jnp.zeros_like(h_ref)

    u = u_ref[...].astype(BF16)
    for c in range(2 * SSM_LANES // 256):
        k0 = ((c % (SSM_LANES // 256)) * 256 // SSM_STATE * SSM_GROUP_CH) // LANE * LANE
        bu_ref[:, c * 256:(c + 1) * 256] = jnp.dot(
            u[:, k0:k0 + LANE], b_ref[k0:k0 + LANE, c * 256:(c + 1) * 256], preferred_element_type=F32)

    cw = 512
    for c in range(SSM_LANES // cw):
        re = slice(c * cw, (c + 1) * cw)
        im = slice(SSM_LANES + c * cw, SSM_LANES + (c + 1) * cw)
        lr = lam_ref[:, re]
        li = lam_ref[:, im]

        def step(t, carry, re=re, im=im, lr=lr, li=li):
            hr, hi = carry
            tt = jnp.where(d == 0, t, TT - 1 - t)
            r = pl.multiple_of(tt * n_batch, n_batch)
            nr = lr * hr - li * hi + bu_ref[pl.ds(r, n_batch), re]
            ni = lr * hi + li * hr + bu_ref[pl.ds(r, n_batch), im]
            hs_ref[pl.ds(r, n_batch), re] = nr
            hs_ref[pl.ds(r, n_batch), im] = ni
            return nr, ni

        hr, hi = lax.fori_loop(0, TT, step, (h_ref[:, re], h_ref[:, im]), unroll=4)
        h_ref[:, re] = hr
        h_ref[:, im] = hi

    half = SSM_LANES // 2
    for j in range(SSM_WIDTH // 256):
        acc = jnp.dot(hs_ref[:, j * half:(j + 1) * half].astype(BF16),
                      c_ref[j * half:(j + 1) * half, j * 256:(j + 1) * 256], preferred_element_type=F32)
        acc += jnp.dot(hs_ref[:, SSM_LANES + j * half:SSM_LANES + (j + 1) * half].astype(BF16),
                       c_ref[SSM_LANES + j * half:SSM_LANES + (j + 1) * half, j * 256:(j + 1) * 256],
                       preferred_element_type=F32)
        y_ref[:, j * 256:(j + 1) * 256] = acc


def _s5_scan(u_rows, b_blk, c_blk, lam_b, n_batch, l_lat, l_ctx):
    rows = TT * n_batch
    n_lat, n_ctx = l_lat // TT, l_ctx // TT
    n_steps = n_lat + n_ctx

    def tile(d, s):
        fwd = jnp.where(s < n_ctx, n_lat + s, s - n_ctx)
        bwd = jnp.where(s < n_ctx, n_lat + n_ctx - 1 - s, n_lat - 1 - (s - n_ctx))
        return jnp.where(d == 0, fwd, bwd)

    kern = functools.partial(_s5_kernel, n_batch=n_batch)
    return pl.pallas_call(
        kern,
        grid=(2, n_steps),
        in_specs=[
            pl.BlockSpec((rows, SSM_WIDTH), lambda d, s: (tile(d, s), 0)),
            pl.BlockSpec((None, SSM_WIDTH, 2 * SSM_LANES), lambda d, s: (d, 0, 0)),
            pl.BlockSpec((None, 2 * SSM_LANES, SSM_WIDTH), lambda d, s: (d, 0, 0)),
            pl.BlockSpec((None, 8, 2 * SSM_LANES), lambda d, s: (d, 0, 0)),
        ],
        out_specs=pl.BlockSpec((None, rows, SSM_WIDTH), lambda d, s: (d, tile(d, s), 0)),
        out_shape=jax.ShapeDtypeStruct((2, (l_lat + l_ctx) * n_batch, SSM_WIDTH), F32),
        scratch_shapes=[
            pltpu.VMEM((rows, 2 * SSM_LANES), F32),
            pltpu.VMEM((rows, 2 * SSM_LANES), F32),
            pltpu.VMEM((n_batch, 2 * SSM_LANES), F32),
        ],
        compiler_params=_cparams(("arbitrary", "arbitrary")),
        name="s5_scan",
    )(u_rows, b_blk, c_blk, lam_b)


def _glu_kernel(u_ref, y_ref, d_ref, w_ref, o_ref, *, n_valid_tiles):
    @pl.when(pl.program_id(0) < n_valid_tiles)
    def _():
        y = u_ref[...] * d_ref[...] + y_ref[0] + y_ref[1]
        g = jax.nn.gelu(y)
        z = jnp.dot(g.astype(BF16), w_ref[...], preferred_element_type=F32)
        o_ref[...] = (g * jax.nn.sigmoid(z)).astype(BF16)

    @pl.when(pl.program_id(0) >= n_valid_tiles)
    def _():
        o_ref[...] = jnp.zeros_like(o_ref)


def _s5_glu(u_rows, y, d_skip, w_glu, n_valid_rows):
    n_rows = u_rows.shape[0]
    tr = 512
    n_valid_tiles = n_valid_rows // tr
    return pl.pallas_call(
        functools.partial(_glu_kernel, n_valid_tiles=n_valid_tiles),
        grid=(n_rows // tr,),
        in_specs=[
            pl.BlockSpec((tr, SSM_WIDTH), lambda i: (i, 0)),
            pl.BlockSpec((2, tr, SSM_WIDTH), lambda i: (0, jnp.minimum(i, n_valid_tiles - 1), 0)),
            pl.BlockSpec((1, SSM_WIDTH), lambda i: (0, 0)),
            pl.BlockSpec((SSM_WIDTH, SSM_WIDTH), lambda i: (0, 0)),
        ],
        out_specs=pl.BlockSpec((tr, SSM_WIDTH), lambda i: (i, 0)),
        out_shape=jax.ShapeDtypeStruct((n_rows, SSM_WIDTH), BF16),
        compiler_params=_cparams(("parallel",)),
        name="s5_glu",
    )(u_rows, y, d_skip, w_glu)


def _pack_bf16_pairs(h):
    n = h.shape[1] // 2
    bits = pltpu.bitcast(h.astype(BF16).astype(F32), jnp.uint32)
    return pltpu.bitcast(bits[:, :n] | (bits[:, n:] >> 16), F32)


def _unpack_bf16_pairs(words):
    bits = pltpu.bitcast(words, jnp.uint32)
    hi = pltpu.bitcast(bits & jnp.uint32(0xFFFF0000), F32).astype(BF16)
    lo = pltpu.bitcast(bits << 16, F32).astype(BF16)
    return hi, lo


def _kout_kernel(da_ref, sw_ref, ss_ref, x_ref, gate_ref, sh_ref, sc_ref, g_ref, w_ref, wr_ref, xh_ref, lg_ref):
    d = g_ref.shape[-1]
    hr = TM // ROW_SPLIT
    for r in range(ROW_SPLIT):
        rows = slice(r * hr, (r + 1) * hr)
        acc = jnp.dot(da_ref[rows, :], w_ref[0:DA_WIDTH, :], preferred_element_type=F32)
        acc += jnp.dot(sw_ref[rows, :], w_ref[DA_WIDTH:DA_WIDTH + SW_WIDTH, :], preferred_element_type=F32)
        acc += jnp.dot(ss_ref[rows, :], w_ref[DA_WIDTH + SW_WIDTH:, :], preferred_element_type=F32)
        x1 = x_ref[rows, :] + gate_ref[...] * acc
        xh_ref[rows, 0:d] = x1
        y = x1 * lax.rsqrt(jnp.mean(x1 * x1, axis=-1, keepdims=True) + NORM_EPS) * g_ref[...]
        h2 = y * (1.0 + sc_ref[...]) + sh_ref[...]
        xh_ref[rows, d:d + d // 2] = _pack_bf16_pairs(h2)
        lg_ref[rows, :] = jnp.dot(h2.astype(BF16), wr_ref[...], preferred_element_type=F32)


def _kout(da, sw, ss, xall, mods4, g2, w_out, layer, w_router, n_lat_tiles):
    b, s_pad = da.shape[0], da.shape[1]
    d = w_out.shape[2]

    def mrow(k):
        return lambda bi, i: (jnp.where(i >= n_lat_tiles, 8, bi), k, 0, 0)

    tok = lambda bi, i: (bi, i, 0)
    return pl.pallas_call(
        _kout_kernel,
        grid=(b, s_pad // TM),
        in_specs=[
            pl.BlockSpec((None, TM, DA_WIDTH), tok),
            pl.BlockSpec((None, TM, SW_WIDTH), tok),
            pl.BlockSpec((TM, SSM_WIDTH), lambda bi, i: (i, bi)),
            pl.BlockSpec((None, TM, d), tok),
            pl.BlockSpec((None, None, 1, d), mrow(2)),
            pl.BlockSpec((None, None, 1, d), mrow(3)),
            pl.BlockSpec((None, None, 1, d), mrow(4)),
            pl.BlockSpec((1, d), lambda bi, i: (0, 0)),
            pl.BlockSpec((None, d, d), lambda bi, i: (layer, 0, 0), pipeline_mode=pl.Buffered(1)),
            pl.BlockSpec((d, LANE), lambda bi, i: (0, 0)),
        ],
        out_specs=[
            pl.BlockSpec((None, TM, d + d // 2), tok),
            pl.BlockSpec((None, TM, LANE), tok),
        ],
        out_shape=[
            jax.ShapeDtypeStruct((b, s_pad, d + d // 2), F32),
            jax.ShapeDtypeStruct((b, s_pad, LANE), F32),
        ],
        compiler_params=_cparams(("parallel", "parallel")),
        name="outproj_norm_router",
    )(da, sw, ss, xall, mods4, mods4, mods4, g2, w_out, w_router)


def _lane_cumsum(x, tri):
    n = x.shape[1]
    total = jnp.zeros((x.shape[0], 1), F32)
    parts = []
    for c in range(n // 256):
        loc = jnp.dot(x[:, c * 256:(c + 1) * 256].astype(BF16), tri, preferred_element_type=F32)
        parts.append(loc + total)
        total = total + loc[:, 255:256]
    return jnp.concatenate(parts, axis=1)


def _route_kernel(lg_ref, tok_ref, idx_ref, gate_ref, *, n_tok, cap):
    lg = lg_ref[...]
    lane = lax.broadcasted_iota(jnp.int32, lg.shape, 1)
    lg = jnp.where(lane < N_EXPERTS, lg, NEG_INF)
    e = jnp.exp(lg - jnp.max(lg, axis=-1, keepdims=True))
    aff = e / jnp.sum(e, axis=-1, keepdims=True)
    aff_t = jnp.transpose(aff)[0:N_EXPERTS, :]
    bits = pltpu.bitcast(aff_t, jnp.int32)

    def search(k, thr):
        cand = thr | lax.shift_left(jnp.int32(1), 30 - k)
        cnt = jnp.sum((bits >= cand).astype(F32), axis=-1, keepdims=True)
        return jnp.where(cnt >= cap, cand, thr)

    thr = lax.fori_loop(0, 31, search, jnp.zeros((N_EXPERTS, 1), jnp.int32))
    gt = bits > thr
    eq = bits == thr
    r_i = lax.broadcasted_iota(jnp.int32, (256, 256), 0)
    c_i = lax.broadcasted_iota(jnp.int32, (256, 256), 1)
    tri = (r_i <= c_i).astype(BF16)
    need = cap - jnp.sum(gt.astype(F32), axis=-1, keepdims=True)
    eq_f = eq.astype(F32)
    eq_rank = _lane_cumsum(eq_f, tri) - eq_f
    sel = jnp.where(gt, 1.0, jnp.where(eq & (eq_rank < need), 1.0, 0.0))
    pos = _lane_cumsum(sel, tri) - sel
    pos = jnp.where(sel > 0.0, pos, -1.0)

    a_hi = aff.astype(BF16)
    r1 = aff - a_hi.astype(F32)
    a_mid = r1.astype(BF16)
    a_lo = (r1 - a_mid.astype(F32)).astype(BF16)
    tok = tok_ref[...]
    payload = jnp.where(lane < 16, a_hi.astype(F32),
                        jnp.where(lane < 32, pltpu.roll(a_mid.astype(F32), 16, 1),
                                  jnp.where(lane < 48, pltpu.roll(a_lo.astype(F32), 32, 1), tok))).astype(BF16)

    slot = lax.broadcasted_iota(jnp.int32, (cap, 256), 0).astype(F32)
    out_lane = lax.broadcasted_iota(jnp.int32, (cap, LANE), 1)
    for ex in range(N_EXPERTS):
        acc = jnp.zeros((cap, LANE), F32)
        for c in range(n_tok // 256):
            onehot = jnp.where(pos[ex:ex + 1, c * 256:(c + 1) * 256] == slot, 1.0, 0.0).astype(BF16)
            acc += jnp.dot(onehot, payload[c * 256:(c + 1) * 256, :], preferred_element_type=F32)
        pick = lambda l: jnp.sum(jnp.where(out_lane == l, acc, 0.0), axis=-1, keepdims=True)
        gate_ref[ex] = (pick(ex) + pick(16 + ex)) + pick(32 + ex)
        idx_ref[ex] = (pick(48) * 64.0 + pick(49)).astype(jnp.int32)


def _route(lg, tok_cols, row_block, n_tok):
    b = lg.shape[0]
    cap = EC_CAPACITY * n_tok // N_EXPERTS
    kern = functools.partial(_route_kernel, n_tok=n_tok, cap=cap)
    idx, gate = pl.pallas_call(
        kern,
        grid=(b,),
        in_specs=[
            pl.BlockSpec((None, n_tok, LANE), lambda bi: (bi, row_block, 0)),
            pl.BlockSpec((n_tok, LANE), lambda bi: (0, 0)),
        ],
        out_specs=[
            pl.BlockSpec((None, N_EXPERTS, cap, 1), lambda bi: (bi, 0, 0, 0)),
            pl.BlockSpec((None, N_EXPERTS, cap, 1), lambda bi: (bi, 0, 0, 0)),
        ],
        out_shape=[
            jax.ShapeDtypeStruct((b, N_EXPERTS, cap, 1), jnp.int32),
            jax.ShapeDtypeStruct((b, N_EXPERTS, cap, 1), F32),
        ],
        compiler_params=_cparams(("parallel",)),
        name="route",
    )(lg, tok_cols)
    return idx[..., 0], gate


def _token_columns(n_tok):
    t = jnp.arange(n_tok, dtype=jnp.int32)[:, None]
    lane = jnp.arange(LANE, dtype=jnp.int32)[None, :]
    return jnp.where(lane == 48, t // 64, jnp.where(lane == 49, t % 64, 0)).astype(F32)


ROW_SLOTS = 3
DMA_UNROLL = 8


def _ffn_kernel(idx_prev_ref, idx_ref, idx_next_ref, gate_ref, m5_ref, m5c_ref, wg_ref, wu_ref, wd_ref, xh_in_hbm,
                xh_hbm, rbuf, sem_in, sem_out, *, n_rows, n_lat_rows, n_batch):
    del xh_in_hbm
    d = wd_ref.shape[-1]
    ff = wg_ref.shape[-1]
    b = pl.program_id(1)
    s = pl.program_id(0) * n_batch + b
    n_steps = N_EXPERTS * n_batch
    slot = s % ROW_SLOTS
    next_slot = (s + 1) % ROW_SLOTS
    prev_slot = (s + 2) % ROW_SLOTS
    next_sample = (b + 1) % n_batch
    prev_sample = (b + n_batch - 1) % n_batch

    def gather_row(idx_r, sample, to_slot, c):
        return pltpu.make_async_copy(xh_hbm.at[sample, pl.ds(idx_r[0, c], 1), :], rbuf.at[to_slot, pl.ds(c, 1), :],
                                     sem_in.at[to_slot])

    def scatter_row(sample, t, from_slot, c):
        return pltpu.make_async_copy(rbuf.at[from_slot, pl.ds(c, 1), pl.ds(0, d)],
                                     xh_hbm.at[sample, pl.ds(t, 1), pl.ds(0, d)], sem_out.at[from_slot])

    def wait_rows_in(at_slot):
        pltpu.make_async_copy(xh_hbm.at[0, pl.ds(0, n_rows), :], rbuf.at[at_slot], sem_in.at[at_slot]).wait()

    def wait_rows_out(at_slot):
        pltpu.make_async_copy(rbuf.at[at_slot, :, pl.ds(0, d)], xh_hbm.at[0, pl.ds(0, n_rows), pl.ds(0, d)],
                              sem_out.at[at_slot]).wait()

    @pl.when(s == 0)
    def _():
        def body(c, carry):
            gather_row(idx_ref, b, slot, c).start()
            gather_row(idx_prev_ref, prev_sample, prev_slot, c).start()
            return carry

        lax.fori_loop(0, n_rows, body, 0, unroll=DMA_UNROLL)
        wait_rows_in(prev_slot)

    @pl.when(s >= 1)
    def _():
        wait_rows_out(next_slot)

    wait_rows_in(slot)

    fc = 256
    n_units = ff // fc + d // fc
    per_unit = -(-n_rows // (3 * n_units // 4))
    issued = [0]

    def issue_row_copies():
        for c in range(issued[0], min(issued[0] + per_unit, n_rows)):
            gather_row(idx_next_ref, next_sample, next_slot, c).start()
            scatter_row(prev_sample, idx_prev_ref[0, c], prev_slot, c).start()
        issued[0] = min(issued[0] + per_unit, n_rows)

    x_hi, x_lo = _unpack_bf16_pairs(rbuf[slot, :, d:d + d // 2])
    hid = []
    for j in range(ff // fc):
        cols = slice(j * fc, (j + 1) * fc)
        hg = (jnp.dot(x_hi, wg_ref[0:d // 2, cols], preferred_element_type=F32)
              + jnp.dot(x_lo, wg_ref[d // 2:d, cols], preferred_element_type=F32))
        hu = (jnp.dot(x_hi, wu_ref[0:d // 2, cols], preferred_element_type=F32)
              + jnp.dot(x_lo, wu_ref[d // 2:d, cols], preferred_element_type=F32))
        hid.append((hg * jax.nn.sigmoid(hg) * hu).astype(BF16))
        issue_row_copies()
    hid = jnp.concatenate(hid, axis=1)
    row = lax.broadcasted_iota(jnp.int32, (n_rows, 1), 0)
    is_lat = row < n_lat_rows
    gate = gate_ref[...]
    for j in range(d // fc):
        cols = slice(j * fc, (j + 1) * fc)
        y = jnp.dot(hid, wd_ref[:, cols], preferred_element_type=F32)
        m5 = jnp.where(is_lat, m5_ref[:, cols], m5c_ref[:, cols])
        rbuf[slot, :, cols] = rbuf[slot, :, cols] + m5 * (y * gate)
        issue_row_copies()
    assert issued[0] == n_rows

    @pl.when(s == n_steps - 1)
    def _():
        wait_rows_in(next_slot)
        wait_rows_out(prev_slot)

        def body(c, carry):
            scatter_row(b, idx_ref[0, c], slot, c).start()
            return carry

        lax.fori_loop(0, n_rows, body, 0, unroll=DMA_UNROLL)
        wait_rows_out(slot)


def _expert_ffn(idx, gate, mods4, w_gate, w_up, w_down, layer, xh, n_lat_rows):
    b, n_xh_rows, row_w = xh.shape
    n_rows = idx.shape[2]
    _, _, d, ff = w_gate.shape
    kern = functools.partial(_ffn_kernel, n_rows=n_rows, n_lat_rows=n_lat_rows, n_batch=b)
    idx4 = idx[:, :, None, :]
    steps = jnp.swapaxes(idx, 0, 1).reshape(N_EXPERTS * b, n_rows)
    idx_prev4 = jnp.concatenate([steps[b - 1:b], steps[:-1]], axis=0).reshape(N_EXPERTS, b, 1, n_rows)

    def next_step(e, bi):
        return (bi + 1) % b, jnp.minimum(e + (bi + 1) // b, N_EXPERTS - 1), 0, 0

    return pl.pallas_call(
        kern,
        grid=(N_EXPERTS, b),
        in_specs=[
            pl.BlockSpec((None, None, 1, n_rows), lambda e, bi: (e, bi, 0, 0), memory_space=pltpu.SMEM),
            pl.BlockSpec((None, None, 1, n_rows), lambda e, bi: (bi, e, 0, 0), memory_space=pltpu.SMEM),
            pl.BlockSpec((None, None, 1, n_rows), next_step, memory_space=pltpu.SMEM),
            pl.BlockSpec((None, None, n_rows, 1), lambda e, bi: (bi, e, 0, 0)),
            pl.BlockSpec((None, None, 1, d), lambda e, bi: (bi, 5, 0, 0)),
            pl.BlockSpec((None, None, 1, d), lambda e, bi: (8, 5, 0, 0)),
            pl.BlockSpec((None, None, d, ff), lambda e, bi: (layer, e, 0, 0)),
            pl.BlockSpec((None, None, d, ff), lambda e, bi: (layer, e, 0, 0)),
            pl.BlockSpec((None, None, ff, d), lambda e, bi: (layer, e, 0, 0)),
            pl.BlockSpec(memory_space=pl.ANY),
        ],
        out_specs=pl.BlockSpec(memory_space=pl.ANY),
        out_shape=jax.ShapeDtypeStruct((b, n_xh_rows, row_w), F32),
        scratch_shapes=[
            pltpu.VMEM((ROW_SLOTS, n_rows, row_w), F32),
            pltpu.SemaphoreType.DMA((ROW_SLOTS,)),
            pltpu.SemaphoreType.DMA((ROW_SLOTS,)),
        ],
        input_output_aliases={9: 0},
        compiler_params=pltpu.CompilerParams(
            dimension_semantics=("arbitrary", "arbitrary"), vmem_limit_bytes=VMEM_LIMIT, has_side_effects=True),
        name="expert_ffn",
    )(idx_prev4, idx4, idx4, gate, mods4, mods4, w_gate, w_up, w_down, xh)


def _final_kernel(x_ref, g_ref, o_ref):
    x = x_ref[...]
    o_ref[...] = x * lax.rsqrt(jnp.mean(x * x, axis=-1, keepdims=True) + NORM_EPS) * g_ref[...]


def _final_norm(xall, g, l_lat):
    b, d = xall.shape[0], g.shape[-1]
    return pl.pallas_call(
        _final_kernel,
        grid=(b, l_lat // TM),
        in_specs=[pl.BlockSpec((None, TM, d), lambda bi, i: (bi, i, 0)), pl.BlockSpec((1, d), lambda bi, i: (0, 0))],
        out_specs=pl.BlockSpec((None, TM, d), lambda bi, i: (bi, i, 0)),
        out_shape=jax.ShapeDtypeStruct((b, l_lat, d), F32),
        compiler_params=_cparams(("parallel", "parallel")),
        name="final_norm",
    )(xall, g)


def _rope_table(l_lat, s_pad, head_dim):
    q = head_dim // 4
    inv_freq = ROPE_THETA ** (-jnp.arange(q, dtype=F32) / q)
    pos = jnp.arange(l_lat)
    ang_r = (pos // GRID_W).astype(F32)[:, None] * inv_freq[None, :]
    ang_c = (pos % GRID_W).astype(F32)[:, None] * inv_freq[None, :]
    ang = jnp.concatenate([ang_r, ang_r, ang_c, ang_c], axis=-1)
    ang = jnp.tile(ang, (1, LANE // head_dim))
    lane = jnp.arange(LANE)[None, :]
    first = (lane % (2 * q)) < q
    cos, sin = jnp.cos(ang), jnp.sin(ang)
    tab = jnp.concatenate([cos, jnp.where(first, -sin, 0.0), jnp.where(first, 0.0, sin)], axis=-1)
    ident = jnp.concatenate([jnp.ones((s_pad - l_lat, LANE), F32), jnp.zeros((s_pad - l_lat, 2 * LANE), F32)], axis=-1)
    return jnp.concatenate([tab, ident], axis=0)


def kernel(x, c, ctx, c_ctx, w_mod, b_mod, norm1_g, norm2_g, w_in, w_out, da_lam_q1, da_lam_k1, da_lam_q2, da_lam_k2, da_subln_g, sw_sink, ssm_lam_re, ssm_lam_im, ssm_log_dt, ssm_b_re, ssm_b_im, ssm_c_re, ssm_c_im, ssm_d, ssm_w_glu, w_router, w_gate, w_up, w_down, final_g):
    b, l_lat, d = x.shape
    l_ctx = ctx.shape[1]
    depth = w_mod.shape[0]
    ctx_pad = -(-l_ctx // TM) * TM
    s_pad = l_lat + ctx_pad
    n_lat_tiles = l_lat // TM
    assert b == 8 and l_lat % TM == 0 and l_lat % GRID_W == 0 and l_ctx % LANE == 0
    assert l_ctx <= min(TQ, DA_TQ) and max(TQ, DA_TQ) <= ctx_pad and l_lat % max(TQ, DA_TQ) == 0
    assert l_lat >= TQ + 2 * SW_WINDOW

    xall = jnp.concatenate([x, ctx, jnp.zeros((b, ctx_pad - l_ctx, d), F32)], axis=1)
    craw = jnp.concatenate([c, c_ctx[None, :], jnp.zeros((16 - b - 1, d), F32)], axis=0)
    mods = _mods(craw, w_mod, b_mod)
    rope_da = _rope_table(l_lat, s_pad, DA_QK)
    rope_sw = _rope_table(l_lat, s_pad, SW_HD)
    rope = jnp.concatenate([rope_da * DA_Q_SCALE, rope_da, rope_sw * SW_Q_SCALE, rope_sw], axis=-1)
    tok_lat = _token_columns(l_lat)
    tok_ctx = _token_columns(l_ctx)
    w_in_b, w_out_b = w_in.astype(BF16), w_out.astype(BF16)
    w_gate_b, w_up_b, w_down_b = w_gate.astype(BF16), w_up.astype(BF16), w_down.astype(BF16)

    for l in range(depth):
        last = l == depth - 1
        lambda_init = 0.8 - 0.6 * math.exp(-0.3 * l)
        mods4 = mods[l].reshape(16, N_MOD, 1, d)
        p, u = _k1(xall, mods4, norm1_g[l][None, :], w_in_b, l, rope, n_lat_tiles)

        lam_params = jnp.zeros((8, LANE), F32).at[0:4, 0:DA_QK].set(
            jnp.stack([da_lam_q1[l], da_lam_k1[l], da_lam_q2[l], da_lam_k2[l]]).astype(F32))
        da = _diff_attention(p, lam_params, da_subln_g[l][None, :].astype(F32), l_lat, l_ctx, lambda_init)
        sw = _window_attention(p, sw_sink[l].astype(F32), l_lat, l_ctx)

        b_blk, c_blk, lam_b = _s5_matrices(ssm_lam_re[l], ssm_lam_im[l], ssm_log_dt[l], ssm_b_re[l], ssm_b_im[l],
                                           ssm_c_re[l], ssm_c_im[l])
        u_rows = u.reshape(s_pad * b, SSM_WIDTH)
        y = _s5_scan(u_rows, b_blk, c_blk, lam_b, b, l_lat, l_ctx)
        ss = _s5_glu(u_rows, y, ssm_d[l][None, :].astype(F32), ssm_w_glu[l].astype(BF16), (l_lat + l_ctx) * b)
        ss = ss.reshape(s_pad, b * SSM_WIDTH)

        w_r = jnp.zeros((d, LANE), BF16).at[:, 0:N_EXPERTS].set(w_router[l].astype(BF16))
        xh, lg = _kout(da, sw, ss, xall, mods4, norm2_g[l][None, :], w_out_b, l, w_r, n_lat_tiles)

        idx, gate = _route(lg, tok_lat, 0, l_lat)
        if not last:
            idx_c, gate_c = _route(lg, tok_ctx, l_lat // l_ctx, l_ctx)
            idx = jnp.concatenate([idx, idx_c + l_lat], axis=2)
            gate = jnp.concatenate([gate, gate_c], axis=2)
        xall = _expert_ffn(idx, gate, mods4, w_gate_b, w_up_b, w_down_b, l, xh, EC_CAPACITY * l_lat // N_EXPERTS)

    return _final_norm(xall, final_g[None, :], l_lat)
```

```python
import functools
import math

import jax
import jax.numpy as jnp
from jax import lax
from jax.experimental import pallas as pl
from jax.experimental.pallas import tpu as pltpu

F32 = jnp.float32
BF16 = jnp.bfloat16

GRID_W = 64
ROPE_THETA = 10000.0
NORM_EPS = 1e-6
NEG_INF = -1e30

DA_HEADS = 6
DA_QK = 64
DA_V = 128
DA_WIDTH = 768
SW_HEADS = 6
SW_KV_HEADS = 2
SW_GROUP = SW_HEADS // SW_KV_HEADS
SW_HD = 128
SW_WINDOW = 128
SW_WIDTH = 768
SSM_GROUPS = 32
SSM_GROUP_CH = 16
SSM_STATE = 64
SSM_WIDTH = 512
SSM_LANES = SSM_GROUPS * SSM_STATE
SSM_RE_MAX = -1e-4
ATT_COLS = 3 * DA_WIDTH + SW_WIDTH + 2 * SW_KV_HEADS * SW_HD
N_EXPERTS = 16
EC_CAPACITY = 2
N_MOD = 6

LOG2_E = math.log2(math.e)
DA_Q_SCALE = DA_QK ** -0.5 * LOG2_E
SW_Q_SCALE = SW_HD ** -0.5 * LOG2_E

TM = 512
TQ = 512
DA_TQ = 512
TT = 64
LANE = 128
VMEM_LIMIT = 56 * 1024 * 1024


def _cparams(sem):
    return pltpu.CompilerParams(dimension_semantics=sem, vmem_limit_bytes=VMEM_LIMIT)


def _mods_kernel(c_ref, w_ref, b_ref, o_ref):
    c = c_ref[...]
    s = (c * jax.nn.sigmoid(c)).astype(BF16)
    o_ref[...] = jnp.dot(s, w_ref[...].astype(BF16), preferred_element_type=F32) + b_ref[...]


def _mods(craw, w_mod, b_mod):
    depth, d, n = w_mod.shape
    tn = 1024
    return pl.pallas_call(
        _mods_kernel,
        grid=(depth, n // tn),
        in_specs=[
            pl.BlockSpec((16, d), lambda l, j: (0, 0)),
            pl.BlockSpec((None, d, tn), lambda l, j: (l, 0, j)),
            pl.BlockSpec((None, 1, tn), lambda l, j: (l, 0, j)),
        ],
        out_specs=pl.BlockSpec((None, 16, tn), lambda l, j: (l, 0, j)),
        out_shape=jax.ShapeDtypeStruct((depth, 16, n), F32),
        compiler_params=_cparams(("parallel", "parallel")),
        name="mods",
    )(craw, w_mod, b_mod.reshape(depth, 1, n))


def _rope(acc, cos, sa, sb, quarter):
    return (acc * cos + pltpu.roll(acc, LANE - quarter, 1) * sa + pltpu.roll(acc, quarter, 1) * sb)


ROW_SPLIT = 2


def _k1_kernel(x_ref, sh_ref, sc_ref, g_ref, w_ref, rope_ref, p_ref, u_ref):
    n_chunks = (ATT_COLS + SSM_WIDTH) // 256
    hr = TM // ROW_SPLIT
    for r in range(ROW_SPLIT):
        rows = slice(r * hr, (r + 1) * hr)
        x = x_ref[rows, :]
        y = x * lax.rsqrt(jnp.mean(x * x, axis=-1, keepdims=True) + NORM_EPS) * g_ref[...]
        h = (y * (1.0 + sc_ref[...]) + sh_ref[...]).astype(BF16)
        for j in range(n_chunks):
            c0 = j * 256
            acc = jnp.dot(h, w_ref[:, c0:c0 + 256], preferred_element_type=F32)
            if c0 >= ATT_COLS:
                u_ref[rows, c0 - ATT_COLS:c0 - ATT_COLS + 256] = acc
                continue
            if c0 < DA_WIDTH:
                t0, quarter = 0, DA_QK // 4
            elif c0 < 2 * DA_WIDTH:
                t0, quarter = 3 * LANE, DA_QK // 4
            elif 3 * DA_WIDTH <= c0 < 3 * DA_WIDTH + SW_WIDTH:
                t0, quarter = 6 * LANE, SW_HD // 4
            elif 3 * DA_WIDTH + SW_WIDTH <= c0 < 3 * DA_WIDTH + SW_WIDTH + SW_KV_HEADS * SW_HD:
                t0, quarter = 9 * LANE, SW_HD // 4
            else:
                t0 = None
            if t0 is not None:
                cos = rope_ref[rows, t0:t0 + LANE]
                sa = rope_ref[rows, t0 + LANE:t0 + 2 * LANE]
                sb = rope_ref[rows, t0 + 2 * LANE:t0 + 3 * LANE]
                acc = jnp.concatenate(
                    [_rope(acc[:, :LANE], cos, sa, sb, quarter), _rope(acc[:, LANE:], cos, sa, sb, quarter)], axis=1)
            p_ref[rows, c0:c0 + 256] = acc.astype(BF16)


def _k1(xall, mods4, g1, w_in, layer, rope, n_lat_tiles):
    b, s_pad = xall.shape[0], rope.shape[0]
    _, d, ncols = w_in.shape

    def mrow(k):
        return lambda bi, i: (jnp.where(i >= n_lat_tiles, 8, bi), k, 0, 0)

    return pl.pallas_call(
        _k1_kernel,
        grid=(b, s_pad // TM),
        in_specs=[
            pl.BlockSpec((None, TM, d), lambda bi, i: (bi, i, 0)),
            pl.BlockSpec((None, None, 1, d), mrow(0)),
            pl.BlockSpec((None, None, 1, d), mrow(1)),
            pl.BlockSpec((1, d), lambda bi, i: (0, 0)),
            pl.BlockSpec((None, d, ncols), lambda bi, i: (layer, 0, 0), pipeline_mode=pl.Buffered(1)),
            pl.BlockSpec((TM, 12 * LANE), lambda bi, i: (i, 0)),
        ],
        out_specs=[
            pl.BlockSpec((None, TM, ATT_COLS), lambda bi, i: (bi, i, 0)),
            pl.BlockSpec((TM, SSM_WIDTH), lambda bi, i: (i, bi)),
        ],
        out_shape=[
            jax.ShapeDtypeStruct((b, s_pad, ATT_COLS), BF16),
            jax.ShapeDtypeStruct((s_pad, b * SSM_WIDTH), F32),
        ],
        compiler_params=_cparams(("parallel", "parallel")),
        name="norm_inproj",
    )(xall, mods4, mods4, g1, w_in, rope)


def _exp2_parts(s):
    m = jnp.max(s, axis=-1, keepdims=True)
    total = jnp.zeros_like(m)
    chunks = []
    for c in range(0, s.shape[1], 256):
        e = jnp.exp2(s[:, c:c + 256] - m)
        total = total + jnp.sum(e, axis=-1, keepdims=True)
        chunks.append(e.astype(BF16))
    return jnp.concatenate(chunks, axis=1), total


def _da_kernel(q_ref, k_ref, v_ref, lp_ref, g_ref, o_ref, *, n_lat_q, l_lat, l_ctx, lambda_init):
    i = pl.program_id(2)
    lp = lp_ref[...]
    lam = (jnp.exp(jnp.sum(lp[0:1] * lp[1:2], axis=-1, keepdims=True))
           - jnp.exp(jnp.sum(lp[2:3] * lp[3:4], axis=-1, keepdims=True)) + lambda_init)

    def attend(k, v):
        q = q_ref[...]
        lane = lax.broadcasted_iota(jnp.int32, q.shape, 1)
        zero = jnp.zeros_like(q)
        dn = (((1,), (1,)), ((), ()))
        e1, l1 = _exp2_parts(lax.dot_general(jnp.where(lane < DA_QK, q, zero), k, dn, preferred_element_type=F32))
        e2, l2 = _exp2_parts(lax.dot_general(jnp.where(lane >= DA_QK, q, zero), k, dn, preferred_element_type=F32))
        o1 = jnp.dot(e1.astype(BF16), v, preferred_element_type=F32)
        o2 = jnp.dot(e2.astype(BF16), v, preferred_element_type=F32)
        o = o1 * (1.0 / l1) - o2 * (lam / l2)
        o = o * lax.rsqrt(jnp.mean(o * o, axis=-1, keepdims=True) + NORM_EPS) * g_ref[...]
        o_ref[...] = (o * (1.0 - lambda_init)).astype(BF16)

    @pl.when(i < n_lat_q)
    def _():
        attend(k_ref[0:l_lat + l_ctx, :], v_ref[0:l_lat + l_ctx, :])

    @pl.when(i == n_lat_q)
    def _():
        attend(k_ref[l_lat:l_lat + l_ctx, :], v_ref[l_lat:l_lat + l_ctx, :])

    @pl.when(i > n_lat_q)
    def _():
        o_ref[...] = jnp.zeros_like(o_ref)


def _diff_attention(p, lam_params, subln_g, l_lat, l_ctx, lambda_init):
    b, s_pad, _ = p.shape
    kern = functools.partial(_da_kernel, n_lat_q=l_lat // DA_TQ, l_lat=l_lat, l_ctx=l_ctx, lambda_init=lambda_init)
    return pl.pallas_call(
        kern,
        grid=(b, DA_HEADS, s_pad // DA_TQ),
        in_specs=[
            pl.BlockSpec((None, DA_TQ, LANE), lambda bi, h, i: (bi, i, h)),
            pl.BlockSpec((None, s_pad, LANE), lambda bi, h, i: (bi, 0, DA_HEADS + h)),
            pl.BlockSpec((None, s_pad, LANE), lambda bi, h, i: (bi, 0, 2 * DA_HEADS + h)),
            pl.BlockSpec((8, LANE), lambda bi, h, i: (0, 0)),
            pl.BlockSpec((1, LANE), lambda bi, h, i: (0, 0)),
        ],
        out_specs=pl.BlockSpec((None, DA_TQ, LANE), lambda bi, h, i: (bi, i, h)),
        out_shape=jax.ShapeDtypeStruct((b, s_pad, DA_WIDTH), BF16),
        compiler_params=_cparams(("parallel", "parallel", "arbitrary")),
        name="diff_attention",
    )(p, p, p, lam_params, subln_g)


def _sw_kernel(sink_ref, q_ref, k_ref, v_ref, o_ref, *, n_lat_q, l_lat, l_ctx):
    i = pl.program_id(2)
    kvh = pl.program_id(1)
    dn = (((1,), (1,)), ((), ()))
    band_w = TQ + 2 * SW_WINDOW
    kc = k_ref[l_lat:l_lat + l_ctx, :]
    vc = v_ref[l_lat:l_lat + l_ctx, :]

    def ctx_scores(qg):
        return lax.dot_general(qg, kc, dn, preferred_element_type=F32)

    @pl.when(i < n_lat_q)
    def _():
        q0 = i * TQ
        kstart = pl.multiple_of(jnp.clip(q0 - SW_WINDOW, 0, l_lat - band_w), SW_WINDOW)
        kb = k_ref[pl.ds(kstart, band_w), :]
        vb = v_ref[pl.ds(kstart, band_w), :]
        qpos = q0 + lax.broadcasted_iota(jnp.int32, (TQ, band_w), 0)
        kpos = kstart + lax.broadcasted_iota(jnp.int32, (TQ, band_w), 1)
        in_band = jnp.abs(qpos - kpos) <= SW_WINDOW
        for g in range(SW_GROUP):
            qg = q_ref[:, g * SW_HD:(g + 1) * SW_HD]
            sink = sink_ref[kvh * SW_GROUP + g] * LOG2_E
            sb = jnp.where(in_band, lax.dot_general(qg, kb, dn, preferred_element_type=F32), NEG_INF)
            sc = ctx_scores(qg)
            m = jnp.maximum(jnp.maximum(jnp.max(sb, axis=-1, keepdims=True), jnp.max(sc, axis=-1, keepdims=True)), sink)
            eb = jnp.exp2(sb - m)
            ec = jnp.exp2(sc - m)
            denom = jnp.sum(eb, axis=-1, keepdims=True) + jnp.sum(ec, axis=-1, keepdims=True) + jnp.exp2(sink - m)
            o = (jnp.dot(eb.astype(BF16), vb, preferred_element_type=F32)
                 + jnp.dot(ec.astype(BF16), vc, preferred_element_type=F32))
            o_ref[:, g * SW_HD:(g + 1) * SW_HD] = (o * (1.0 / denom)).astype(BF16)

    @pl.when(i == n_lat_q)
    def _():
        for g in range(SW_GROUP):
            qg = q_ref[:, g * SW_HD:(g + 1) * SW_HD]
            sink = sink_ref[kvh * SW_GROUP + g] * LOG2_E
            sc = ctx_scores(qg)
            m = jnp.maximum(jnp.max(sc, axis=-1, keepdims=True), sink)
            ec = jnp.exp2(sc - m)
            denom = jnp.sum(ec, axis=-1, keepdims=True) + jnp.exp2(sink - m)
            o = jnp.dot(ec.astype(BF16), vc, preferred_element_type=F32)
            o_ref[:, g * SW_HD:(g + 1) * SW_HD] = (o * (1.0 / denom)).astype(BF16)

    @pl.when(i > n_lat_q)
    def _():
        o_ref[...] = jnp.zeros_like(o_ref)


def _window_attention(p, sink, l_lat, l_ctx):
    b, s_pad, _ = p.shape
    gw = SW_GROUP * SW_HD
    q_blk0 = 3 * DA_WIDTH // gw
    k_blk0 = (3 * DA_WIDTH + SW_WIDTH) // SW_HD
    v_blk0 = k_blk0 + SW_KV_HEADS
    kern = functools.partial(_sw_kernel, n_lat_q=l_lat // TQ, l_lat=l_lat, l_ctx=l_ctx)
    return pl.pallas_call(
        kern,
        grid=(b, SW_KV_HEADS, s_pad // TQ),
        in_specs=[
            pl.BlockSpec(memory_space=pltpu.SMEM),
            pl.BlockSpec((None, TQ, gw), lambda bi, h, i: (bi, i, q_blk0 + h)),
            pl.BlockSpec((None, s_pad, SW_HD), lambda bi, h, i: (bi, 0, k_blk0 + h)),
            pl.BlockSpec((None, s_pad, SW_HD), lambda bi, h, i: (bi, 0, v_blk0 + h)),
        ],
        out_specs=pl.BlockSpec((None, TQ, gw), lambda bi, h, i: (bi, i, h)),
        out_shape=jax.ShapeDtypeStruct((b, s_pad, SW_WIDTH), BF16),
        compiler_params=_cparams(("parallel", "parallel", "arbitrary")),
        name="window_attention",
    )(sink, p, p, p)


def _s5_discretise(lam_re, lam_im, log_dt, b_re, b_im):
    lr = jnp.minimum(lam_re.astype(F32), SSM_RE_MAX)
    li = lam_im.astype(F32)
    dt = jnp.exp(log_dt.astype(F32))[..., None]
    mag = jnp.exp(lr * dt)
    lbr = mag * jnp.cos(li * dt)
    lbi = mag * jnp.sin(li * dt)
    den = lr * lr + li * li
    cr = ((lbr - 1.0) * lr + lbi * li) / den
    ci = (lbi * lr - (lbr - 1.0) * li) / den
    br = b_re.astype(F32)
    bi = b_im.astype(F32)
    bbr = cr[..., None] * br - ci[..., None] * bi
    bbi = cr[..., None] * bi + ci[..., None] * br
    return lbr, lbi, bbr, bbi


def _s5_matrices(lam_re, lam_im, log_dt, b_re, b_im, c_re, c_im):
    lbr, lbi, bbr, bbi = _s5_discretise(lam_re, lam_im, log_dt, b_re, b_im)
    eye = jnp.eye(SSM_GROUPS, dtype=F32)

    def in_map(bb):
        return jnp.einsum("dgnp,gh->dgphn", bb, eye).reshape(2, SSM_WIDTH, SSM_LANES)

    def out_map(cc):
        return jnp.einsum("dgpn,gh->dgnhp", cc, eye).reshape(2, SSM_LANES, SSM_WIDTH)

    b_blk = jnp.concatenate([in_map(bbr), in_map(bbi)], axis=2).astype(BF16)
    c_blk = jnp.concatenate([out_map(c_re.astype(F32)), out_map(-c_im.astype(F32))], axis=1).astype(BF16)
    lam_b = jnp.concatenate([lbr.reshape(2, 1, SSM_LANES), lbi.reshape(2, 1, SSM_LANES)], axis=2)
    lam_b = jnp.broadcast_to(lam_b, (2, 8, 2 * SSM_LANES))
    return b_blk, c_blk, lam_b


def _s5_kernel(u_ref, b_ref, c_ref, lam_ref, y_ref, bu_ref, hs_ref, h_ref, *, n_batch):
    d = pl.program_id(0)
    s = pl.program_id(1)
    rows = TT * n_batch

    @pl.when(s == 0)
    def _():
        h_ref[...] = jnp.zeros_like(h_ref)

    u = u_ref[...].astype(BF16)
    for c in range(2 * SSM_LANES // 256):
        k0 = ((c % (SSM_LANES // 256)) * 256 // SSM_STATE * SSM_GROUP_CH) // LANE * LANE
        bu_ref[:, c * 256:(c + 1) * 256] = jnp.dot(
            u[:, k0:k0 + LANE], b_ref[k0:k0 + LANE, c * 256:(c + 1) * 256], preferred_element_type=F32)

    cw = 512
    for c in range(SSM_LANES // cw):
        re = slice(c * cw, (c + 1) * cw)
        im = slice(SSM_LANES + c * cw, SSM_LANES + (c + 1) * cw)
        lr = lam_ref[:, re]
        li = lam_ref[:, im]

        def step(t, carry, re=re, im=im, lr=lr, li=li):
            hr, hi = carry
            tt = jnp.where(d == 0, t, TT - 1 - t)
            r = pl.multiple_of(tt * n_batch, n_batch)
            nr = lr * hr - li * hi + bu_ref[pl.ds(r, n_batch), re]
            ni = lr * hi + li * hr + bu_ref[pl.ds(r, n_batch), im]
            hs_ref[pl.ds(r, n_batch), re] = nr
            hs_ref[pl.ds(r, n_batch), im] = ni
            return nr, ni

        hr, hi = lax.fori_loop(0, TT, step, (h_ref[:, re], h_ref[:, im]), unroll=4)
        h_ref[:, re] = hr
        h_ref[:, im] = hi

    half = SSM_LANES // 2
    for j in range(SSM_WIDTH // 256):
        acc = jnp.dot(hs_ref[:, j * half:(j + 1) * half].astype(BF16),
                      c_ref[j * half:(j + 1) * half, j * 256:(j + 1) * 256], preferred_element_type=F32)
        acc += jnp.dot(hs_ref[:, SSM_LANES + j * half:SSM_LANES + (j + 1) * half].astype(BF16),
                       c_ref[SSM_LANES + j * half:SSM_LANES + (j + 1) * half, j * 256:(j + 1) * 256],
                       preferred_element_type=F32)
        y_ref[:, j * 256:(j + 1) * 256] = acc


def _s5_scan(u_rows, b_blk, c_blk, lam_b, n_batch, l_lat, l_ctx):
    rows = TT * n_batch
    n_lat, n_ctx = l_lat // TT, l_ctx // TT
    n_steps = n_lat + n_ctx

    def tile(d, s):
        fwd = jnp.where(s < n_ctx, n_lat + s, s - n_ctx)
        bwd = jnp.where(s < n_ctx, n_lat + n_ctx - 1 - s, n_lat - 1 - (s - n_ctx))
        return jnp.where(d == 0, fwd, bwd)

    kern = functools.partial(_s5_kernel, n_batch=n_batch)
    return pl.pallas_call(
        kern,
        grid=(2, n_steps),
        in_specs=[
            pl.BlockSpec((rows, SSM_WIDTH), lambda d, s: (tile(d, s), 0)),
            pl.BlockSpec((None, SSM_WIDTH, 2 * SSM_LANES), lambda d, s: (d, 0, 0)),
            pl.BlockSpec((None, 2 * SSM_LANES, SSM_WIDTH), lambda d, s: (d, 0, 0)),
            pl.BlockSpec((None, 8, 2 * SSM_LANES), lambda d, s: (d, 0, 0)),
        ],
        out_specs=pl.BlockSpec((None, rows, SSM_WIDTH), lambda d, s: (d, tile(d, s), 0)),
        out_shape=jax.ShapeDtypeStruct((2, (l_lat + l_ctx) * n_batch, SSM_WIDTH), F32),
        scratch_shapes=[
            pltpu.VMEM((rows, 2 * SSM_LANES), F32),
            pltpu.VMEM((rows, 2 * SSM_LANES), F32),
            pltpu.VMEM((n_batch, 2 * SSM_LANES), F32),
        ],
        compiler_params=_cparams(("arbitrary", "arbitrary")),
        name="s5_scan",
    )(u_rows, b_blk, c_blk, lam_b)


def _glu_kernel(u_ref, y_ref, d_ref, w_ref, o_ref, *, n_valid_tiles):
    @pl.when(pl.program_id(0) < n_valid_tiles)
    def _():
        y = u_ref[...] * d_ref[...] + y_ref[0] + y_ref[1]
        g = jax.nn.gelu(y)
        z = jnp.dot(g.astype(BF16), w_ref[...], preferred_element_type=F32)
        o_ref[...] = (g * jax.nn.sigmoid(z)).astype(BF16)

    @pl.when(pl.program_id(0) >= n_valid_tiles)
    def _():
        o_ref[...] = jnp.zeros_like(o_ref)


def _s5_glu(u_rows, y, d_skip, w_glu, n_valid_rows):
    n_rows = u_rows.shape[0]
    tr = 512
    n_valid_tiles = n_valid_rows // tr
    return pl.pallas_call(
        functools.partial(_glu_kernel, n_valid_tiles=n_valid_tiles),
        grid=(n_rows // tr,),
        in_specs=[
            pl.BlockSpec((tr, SSM_WIDTH), lambda i: (i, 0)),
            pl.BlockSpec((2, tr, SSM_WIDTH), lambda i: (0, jnp.minimum(i, n_valid_tiles - 1), 0)),
            pl.BlockSpec((1, SSM_WIDTH), lambda i: (0, 0)),
            pl.BlockSpec((SSM_WIDTH, SSM_WIDTH), lambda i: (0, 0)),
        ],
        out_specs=pl.BlockSpec((tr, SSM_WIDTH), lambda i: (i, 0)),
        out_shape=jax.ShapeDtypeStruct((n_rows, SSM_WIDTH), BF16),
        compiler_params=_cparams(("parallel",)),
        name="s5_glu",
    )(u_rows, y, d_skip, w_glu)


def _pack_bf16_pairs(h):
    n = h.shape[1] // 2
    bits = pltpu.bitcast(h.astype(BF16).astype(F32), jnp.uint32)
    return pltpu.bitcast(bits[:, :n] | (bits[:, n:] >> 16), F32)


def _unpack_bf16_pairs(words):
    bits = pltpu.bitcast(words, jnp.uint32)
    hi = pltpu.bitcast(bits & jnp.uint32(0xFFFF0000), F32).astype(BF16)
    lo = pltpu.bitcast(bits << 16, F32).astype(BF16)
    return hi, lo


def _kout_kernel(da_ref, sw_ref, ss_ref, x_ref, gate_ref, sh_ref, sc_ref, g_ref, w_ref, wr_ref, xh_ref, lg_ref):
    d = g_ref.shape[-1]
    hr = TM // ROW_SPLIT
    for r in range(ROW_SPLIT):
        rows = slice(r * hr, (r + 1) * hr)
        acc = jnp.dot(da_ref[rows, :], w_ref[0:DA_WIDTH, :], preferred_element_type=F32)
        acc += jnp.dot(sw_ref[rows, :], w_ref[DA_WIDTH:DA_WIDTH + SW_WIDTH, :], preferred_element_type=F32)
        acc += jnp.dot(ss_ref[rows, :], w_ref[DA_WIDTH + SW_WIDTH:, :], preferred_element_type=F32)
        x1 = x_ref[rows, :] + gate_ref[...] * acc
        xh_ref[rows, 0:d] = x1
        y = x1 * lax.rsqrt(jnp.mean(x1 * x1, axis=-1, keepdims=True) + NORM_EPS) * g_ref[...]
        h2 = y * (1.0 + sc_ref[...]) + sh_ref[...]
        xh_ref[rows, d:d + d // 2] = _pack_bf16_pairs(h2)
        lg_ref[rows, :] = jnp.dot(h2.astype(BF16), wr_ref[...], preferred_element_type=F32)


def _kout(da, sw, ss, xall, mods4, g2, w_out, layer, w_router, n_lat_tiles):
    b, s_pad = da.shape[0], da.shape[1]
    d = w_out.shape[2]

    def mrow(k):
        return lambda bi, i: (jnp.where(i >= n_lat_tiles, 8, bi), k, 0, 0)

    tok = lambda bi, i: (bi, i, 0)
    return pl.pallas_call(
        _kout_kernel,
        grid=(b, s_pad // TM),
        in_specs=[
            pl.BlockSpec((None, TM, DA_WIDTH), tok),
            pl.BlockSpec((None, TM, SW_WIDTH), tok),
            pl.BlockSpec((TM, SSM_WIDTH), lambda bi, i: (i, bi)),
            pl.BlockSpec((None, TM, d), tok),
            pl.BlockSpec((None, None, 1, d), mrow(2)),
            pl.BlockSpec((None, None, 1, d), mrow(3)),
            pl.BlockSpec((None, None, 1, d), mrow(4)),
            pl.BlockSpec((1, d), lambda bi, i: (0, 0)),
            pl.BlockSpec((None, d, d), lambda bi, i: (layer, 0, 0), pipeline_mode=pl.Buffered(1)),
            pl.BlockSpec((d, LANE), lambda bi, i: (0, 0)),
        ],
        out_specs=[
            pl.BlockSpec((None, TM, d + d // 2), tok),
            pl.BlockSpec((None, TM, LANE), tok),
        ],
        out_shape=[
            jax.ShapeDtypeStruct((b, s_pad, d + d // 2), F32),
            jax.ShapeDtypeStruct((b, s_pad, LANE), F32),
        ],
        compiler_params=_cparams(("parallel", "parallel")),
        name="outproj_norm_router",
    )(da, sw, ss, xall, mods4, mods4, mods4, g2, w_out, w_router)


def _lane_cumsum(x, tri):
    n = x.shape[1]
    total = jnp.zeros((x.shape[0], 1), F32)
    parts = []
    for c in range(n // 256):
        loc = jnp.dot(x[:, c * 256:(c + 1) * 256].astype(BF16), tri, preferred_element_type=F32)
        parts.append(loc + total)
        total = total + loc[:, 255:256]
    return jnp.concatenate(parts, axis=1)


def _route_kernel(lg_ref, tok_ref, idx_ref, gate_ref, *, n_tok, cap):
    lg = lg_ref[...]
    lane = lax.broadcasted_iota(jnp.int32, lg.shape, 1)
    lg = jnp.where(lane < N_EXPERTS, lg, NEG_INF)
    e = jnp.exp(lg - jnp.max(lg, axis=-1, keepdims=True))
    aff = e / jnp.sum(e, axis=-1, keepdims=True)
    aff_t = jnp.transpose(aff)[0:N_EXPERTS, :]
    bits = pltpu.bitcast(aff_t, jnp.int32)

    def search(k, thr):
        cand = thr | lax.shift_left(jnp.int32(1), 30 - k)
        cnt = jnp.sum((bits >= cand).astype(F32), axis=-1, keepdims=True)
        return jnp.where(cnt >= cap, cand, thr)

    thr = lax.fori_loop(0, 31, search, jnp.zeros((N_EXPERTS, 1), jnp.int32))
    gt = bits > thr
    eq = bits == thr
    r_i = lax.broadcasted_iota(jnp.int32, (256, 256), 0)
    c_i = lax.broadcasted_iota(jnp.int32, (256, 256), 1)
    tri = (r_i <= c_i).astype(BF16)
    need = cap - jnp.sum(gt.astype(F32), axis=-1, keepdims=True)
    eq_f = eq.astype(F32)
    eq_rank = _lane_cumsum(eq_f, tri) - eq_f
    sel = jnp.where(gt, 1.0, jnp.where(eq & (eq_rank < need), 1.0, 0.0))
    pos = _lane_cumsum(sel, tri) - sel
    pos = jnp.where(sel > 0.0, pos, -1.0)

    a_hi = aff.astype(BF16)
    r1 = aff - a_hi.astype(F32)
    a_mid = r1.astype(BF16)
    a_lo = (r1 - a_mid.astype(F32)).astype(BF16)
    tok = tok_ref[...]
    payload = jnp.where(lane < 16, a_hi.astype(F32),
                        jnp.where(lane < 32, pltpu.roll(a_mid.astype(F32), 16, 1),
                                  jnp.where(lane < 48, pltpu.roll(a_lo.astype(F32), 32, 1), tok))).astype(BF16)

    slot = lax.broadcasted_iota(jnp.int32, (cap, 256), 0).astype(F32)
    out_lane = lax.broadcasted_iota(jnp.int32, (cap, LANE), 1)
    for ex in range(N_EXPERTS):
        acc = jnp.zeros((cap, LANE), F32)
        for c in range(n_tok // 256):
            onehot = jnp.where(pos[ex:ex + 1, c * 256:(c + 1) * 256] == slot, 1.0, 0.0).astype(BF16)
            acc += jnp.dot(onehot, payload[c * 256:(c + 1) * 256, :], preferred_element_type=F32)
        pick = lambda l: jnp.sum(jnp.where(out_lane == l, acc, 0.0), axis=-1, keepdims=True)
        gate_ref[ex] = (pick(ex) + pick(16 + ex)) + pick(32 + ex)
        idx_ref[ex] = (pick(48) * 64.0 + pick(49)).astype(jnp.int32)


def _route(lg, tok_cols, row_block, n_tok):
    b = lg.shape[0]
    cap = EC_CAPACITY * n_tok // N_EXPERTS
    kern = functools.partial(_route_kernel, n_tok=n_tok, cap=cap)
    idx, gate = pl.pallas_call(
        kern,
        grid=(b,),
        in_specs=[
            pl.BlockSpec((None, n_tok, LANE), lambda bi: (bi, row_block, 0)),
            pl.BlockSpec((n_tok, LANE), lambda bi: (0, 0)),
        ],
        out_specs=[
            pl.BlockSpec((None, N_EXPERTS, cap, 1), lambda bi: (bi, 0, 0, 0)),
            pl.BlockSpec((None, N_EXPERTS, cap, 1), lambda bi: (bi, 0, 0, 0)),
        ],
        out_shape=[
            jax.ShapeDtypeStruct((b, N_EXPERTS, cap, 1), jnp.int32),
            jax.ShapeDtypeStruct((b, N_EXPERTS, cap, 1), F32),
        ],
        compiler_params=_cparams(("parallel",)),
        name="route",
    )(lg, tok_cols)
    return idx[..., 0], gate


def _token_columns(n_tok):
    t = jnp.arange(n_tok, dtype=jnp.int32)[:, None]
    lane = jnp.arange(LANE, dtype=jnp.int32)[None, :]
    return jnp.where(lane == 48, t // 64, jnp.where(lane == 49, t % 64, 0)).astype(F32)


ROW_SLOTS = 3
DMA_UNROLL = 8


def _ffn_kernel(idx_prev_ref, idx_ref, idx_next_ref, gate_ref, m5_ref, m5c_ref, wg_ref, wu_ref, wd_ref, xh_in_hbm,
                xh_hbm, rbuf, sem_in, sem_out, *, n_rows, n_lat_rows, n_batch):
    del xh_in_hbm
    d = wd_ref.shape[-1]
    ff = wg_ref.shape[-1]
    b = pl.program_id(1)
    s = pl.program_id(0) * n_batch + b
    n_steps = N_EXPERTS * n_batch
    slot = s % ROW_SLOTS
    next_slot = (s + 1) % ROW_SLOTS
    prev_slot = (s + 2) % ROW_SLOTS
    next_sample = (b + 1) % n_batch
    prev_sample = (b + n_batch - 1) % n_batch

    def gather_row(idx_r, sample, to_slot, c):
        return pltpu.make_async_copy(xh_hbm.at[sample, pl.ds(idx_r[0, c], 1), :], rbuf.at[to_slot, pl.ds(c, 1), :],
                                     sem_in.at[to_slot])

    def scatter_row(sample, t, from_slot, c):
        return pltpu.make_async_copy(rbuf.at[from_slot, pl.ds(c, 1), pl.ds(0, d)],
                                     xh_hbm.at[sample, pl.ds(t, 1), pl.ds(0, d)], sem_out.at[from_slot])

    def wait_rows_in(at_slot):
        pltpu.make_async_copy(xh_hbm.at[0, pl.ds(0, n_rows), :], rbuf.at[at_slot], sem_in.at[at_slot]).wait()

    def wait_rows_out(at_slot):
        pltpu.make_async_copy(rbuf.at[at_slot, :, pl.ds(0, d)], xh_hbm.at[0, pl.ds(0, n_rows), pl.ds(0, d)],
                              sem_out.at[at_slot]).wait()

    @pl.when(s == 0)
    def _():
        def body(c, carry):
            gather_row(idx_ref, b, slot, c).start()
            gather_row(idx_prev_ref, prev_sample, prev_slot, c).start()
            return carry

        lax.fori_loop(0, n_rows, body, 0, unroll=DMA_UNROLL)
        wait_rows_in(prev_slot)

    @pl.when(s >= 1)
    def _():
        wait_rows_out(next_slot)

    wait_rows_in(slot)

    fc = 256
    n_units = ff // fc + d // fc
    per_unit = -(-n_rows // (3 * n_units // 4))
    issued = [0]

    def issue_row_copies():
        for c in range(issued[0], min(issued[0] + per_unit, n_rows)):
            gather_row(idx_next_ref, next_sample, next_slot, c).start(priority=c % 2)
            scatter_row(prev_sample, idx_prev_ref[0, c], prev_slot, c).start(priority=(c + 1) % 2)
        issued[0] = min(issued[0] + per_unit, n_rows)

    x_hi, x_lo = _unpack_bf16_pairs(rbuf[slot, :, d:d + d // 2])
    hid = []
    for j in range(ff // fc):
        cols = slice(j * fc, (j + 1) * fc)
        hg = (jnp.dot(x_hi, wg_ref[0:d // 2, cols], preferred_element_type=F32)
              + jnp.dot(x_lo, wg_ref[d // 2:d, cols], preferred_element_type=F32))
        hu = (jnp.dot(x_hi, wu_ref[0:d // 2, cols], preferred_element_type=F32)
              + jnp.dot(x_lo, wu_ref[d // 2:d, cols], preferred_element_type=F32))
        hid.append((hg * jax.nn.sigmoid(hg) * hu).astype(BF16))
        issue_row_copies()
    hid = jnp.concatenate(hid, axis=1)
    row = lax.broadcasted_iota(jnp.int32, (n_rows, 1), 0)
    is_lat = row < n_lat_rows
    gate = gate_ref[...]
    for j in range(d // fc):
        cols = slice(j * fc, (j + 1) * fc)
        y = jnp.dot(hid, wd_ref[:, cols], preferred_element_type=F32)
        m5 = jnp.where(is_lat, m5_ref[:, cols], m5c_ref[:, cols])
        rbuf[slot, :, cols] = rbuf[slot, :, cols] + m5 * (y * gate)
        issue_row_copies()
    assert issued[0] == n_rows

    @pl.when(s == n_steps - 1)
    def _():
        wait_rows_in(next_slot)
        wait_rows_out(prev_slot)

        def body(c, carry):
            scatter_row(b, idx_ref[0, c], slot, c).start()
            return carry

        lax.fori_loop(0, n_rows, body, 0, unroll=DMA_UNROLL)
        wait_rows_out(slot)


def _expert_ffn(idx, gate, mods4, w_gate, w_up, w_down, layer, xh, n_lat_rows):
    b, n_xh_rows, row_w = xh.shape
    n_rows = idx.shape[2]
    _, _, d, ff = w_gate.shape
    kern = functools.partial(_ffn_kernel, n_rows=n_rows, n_lat_rows=n_lat_rows, n_batch=b)
    idx4 = idx[:, :, None, :]
    steps = jnp.swapaxes(idx, 0, 1).reshape(N_EXPERTS * b, n_rows)
    idx_prev4 = jnp.concatenate([steps[b - 1:b], steps[:-1]], axis=0).reshape(N_EXPERTS, b, 1, n_rows)

    def next_step(e, bi):
        return (bi + 1) % b, jnp.minimum(e + (bi + 1) // b, N_EXPERTS - 1), 0, 0

    return pl.pallas_call(
        kern,
        grid=(N_EXPERTS, b),
        in_specs=[
            pl.BlockSpec((None, None, 1, n_rows), lambda e, bi: (e, bi, 0, 0), memory_space=pltpu.SMEM),
            pl.BlockSpec((None, None, 1, n_rows), lambda e, bi: (bi, e, 0, 0), memory_space=pltpu.SMEM),
            pl.BlockSpec((None, None, 1, n_rows), next_step, memory_space=pltpu.SMEM),
            pl.BlockSpec((None, None, n_rows, 1), lambda e, bi: (bi, e, 0, 0)),
            pl.BlockSpec((None, None, 1, d), lambda e, bi: (bi, 5, 0, 0)),
            pl.BlockSpec((None, None, 1, d), lambda e, bi: (8, 5, 0, 0)),
            pl.BlockSpec((None, None, d, ff), lambda e, bi: (layer, e, 0, 0)),
            pl.BlockSpec((None, None, d, ff), lambda e, bi: (layer, e, 0, 0)),
            pl.BlockSpec((None, None, ff, d), lambda e, bi: (layer, e, 0, 0)),
            pl.BlockSpec(memory_space=pl.ANY),
        ],
        out_specs=pl.BlockSpec(memory_space=pl.ANY),
        out_shape=jax.ShapeDtypeStruct((b, n_xh_rows, row_w), F32),
        scratch_shapes=[
            pltpu.VMEM((ROW_SLOTS, n_rows, row_w), F32),
            pltpu.SemaphoreType.DMA((ROW_SLOTS,)),
            pltpu.SemaphoreType.DMA((ROW_SLOTS,)),
        ],
        input_output_aliases={9: 0},
        compiler_params=pltpu.CompilerParams(
            dimension_semantics=("arbitrary", "arbitrary"), vmem_limit_bytes=VMEM_LIMIT, has_side_effects=True),
        name="expert_ffn",
    )(idx_prev4, idx4, idx4, gate, mods4, mods4, w_gate, w_up, w_down, xh)


def _final_kernel(x_ref, g_ref, o_ref):
    x = x_ref[...]
    o_ref[...] = x * lax.rsqrt(jnp.mean(x * x, axis=-1, keepdims=True) + NORM_EPS) * g_ref[...]


def _final_norm(xall, g, l_lat):
    b, d = xall.shape[0], g.shape[-1]
    return pl.pallas_call(
        _final_kernel,
        grid=(b, l_lat // TM),
        in_specs=[pl.BlockSpec((None, TM, d), lambda bi, i: (bi, i, 0)), pl.BlockSpec((1, d), lambda bi, i: (0, 0))],
        out_specs=pl.BlockSpec((None, TM, d), lambda bi, i: (bi, i, 0)),
        out_shape=jax.ShapeDtypeStruct((b, l_lat, d), F32),
        compiler_params=_cparams(("parallel", "parallel")),
        name="final_norm",
    )(xall, g)


def _rope_table(l_lat, s_pad, head_dim):
    q = head_dim // 4
    inv_freq = ROPE_THETA ** (-jnp.arange(q, dtype=F32) / q)
    pos = jnp.arange(l_lat)
    ang_r = (pos // GRID_W).astype(F32)[:, None] * inv_freq[None, :]
    ang_c = (pos % GRID_W).astype(F32)[:, None] * inv_freq[None, :]
    ang = jnp.concatenate([ang_r, ang_r, ang_c, ang_c], axis=-1)
    ang = jnp.tile(ang, (1, LANE // head_dim))
    lane = jnp.arange(LANE)[None, :]
    first = (lane % (2 * q)) < q
    cos, sin = jnp.cos(ang), jnp.sin(ang)
    tab = jnp.concatenate([cos, jnp.where(first, -sin, 0.0), jnp.where(first, 0.0, sin)], axis=-1)
    ident = jnp.concatenate([jnp.ones((s_pad - l_lat, LANE), F32), jnp.zeros((s_pad - l_lat, 2 * LANE), F32)], axis=-1)
    return jnp.concatenate([tab, ident], axis=0)


def kernel(x, c, ctx, c_ctx, w_mod, b_mod, norm1_g, norm2_g, w_in, w_out, da_lam_q1, da_lam_k1, da_lam_q2, da_lam_k2, da_subln_g, sw_sink, ssm_lam_re, ssm_lam_im, ssm_log_dt, ssm_b_re, ssm_b_im, ssm_c_re, ssm_c_im, ssm_d, ssm_w_glu, w_router, w_gate, w_up, w_down, final_g):
    b, l_lat, d = x.shape
    l_ctx = ctx.shape[1]
    depth = w_mod.shape[0]
    ctx_pad = -(-l_ctx // TM) * TM
    s_pad = l_lat + ctx_pad
    n_lat_tiles = l_lat // TM
    assert b == 8 and l_lat % TM == 0 and l_lat % GRID_W == 0 and l_ctx % LANE == 0
    assert l_ctx <= min(TQ, DA_TQ) and max(TQ, DA_TQ) <= ctx_pad and l_lat % max(TQ, DA_TQ) == 0
    assert l_lat >= TQ + 2 * SW_WINDOW

    xall = jnp.concatenate([x, ctx, jnp.zeros((b, ctx_pad - l_ctx, d), F32)], axis=1)
    craw = jnp.concatenate([c, c_ctx[None, :], jnp.zeros((16 - b - 1, d), F32)], axis=0)
    mods = _mods(craw, w_mod, b_mod)
    rope_da = _rope_table(l_lat, s_pad, DA_QK)
    rope_sw = _rope_table(l_lat, s_pad, SW_HD)
    rope = jnp.concatenate([rope_da * DA_Q_SCALE, rope_da, rope_sw * SW_Q_SCALE, rope_sw], axis=-1)
    tok_lat = _token_columns(l_lat)
    tok_ctx = _token_columns(l_ctx)
    w_in_b, w_out_b = w_in.astype(BF16), w_out.astype(BF16)
    w_gate_b, w_up_b, w_down_b = w_gate.astype(BF16), w_up.astype(BF16), w_down.astype(BF16)

    for l in range(depth):
        last = l == depth - 1
        lambda_init = 0.8 - 0.6 * math.exp(-0.3 * l)
        mods4 = mods[l].reshape(16, N_MOD, 1, d)
        p, u = _k1(xall, mods4, norm1_g[l][None, :], w_in_b, l, rope, n_lat_tiles)

        lam_params = jnp.zeros((8, LANE), F32).at[0:4, 0:DA_QK].set(
            jnp.stack([da_lam_q1[l], da_lam_k1[l], da_lam_q2[l], da_lam_k2[l]]).astype(F32))
        da = _diff_attention(p, lam_params, da_subln_g[l][None, :].astype(F32), l_lat, l_ctx, lambda_init)
        sw = _window_attention(p, sw_sink[l].astype(F32), l_lat, l_ctx)

        b_blk, c_blk, lam_b = _s5_matrices(ssm_lam_re[l], ssm_lam_im[l], ssm_log_dt[l], ssm_b_re[l], ssm_b_im[l],
                                           ssm_c_re[l], ssm_c_im[l])
        u_rows = u.reshape(s_pad * b, SSM_WIDTH)
        y = _s5_scan(u_rows, b_blk, c_blk, lam_b, b, l_lat, l_ctx)
        ss = _s5_glu(u_rows, y, ssm_d[l][None, :].astype(F32), ssm_w_glu[l].astype(BF16), (l_lat + l_ctx) * b)
        ss = ss.reshape(s_pad, b * SSM_WIDTH)

        w_r = jnp.zeros((d, LANE), BF16).at[:, 0:N_EXPERTS].set(w_router[l].astype(BF16))
        xh, lg = _kout(da, sw, ss, xall, mods4, norm2_g[l][None, :], w_out_b, l, w_r, n_lat_tiles)

        idx, gate = _route(lg, tok_lat, 0, l_lat)
        if not last:
            idx_c, gate_c = _route(lg, tok_ctx, l_lat // l_ctx, l_ctx)
            idx = jnp.concatenate([idx, idx_c + l_lat], axis=2)
            gate = jnp.concatenate([gate, gate_c], axis=2)
        xall = _expert_ffn(idx, gate, mods4, w_gate_b, w_up_b, w_down_b, l, xh, EC_CAPACITY * l_lat // N_EXPERTS)

    return _final_norm(xall, final_g[None, :], l_lat)
```
